```python
import jax, jax.numpy as jnp
from jax import lax
import numpy as np

D_MODEL = 1024
BATCH = 8
SEQ = 2048
DEPTH = 2

GRID_W = 64
CTX_LEN = 256
N_EVEN = (DEPTH + 1) // 2
N_ODD = DEPTH // 2

MLA_HEADS = 8
MLA_Q_RANK = 384
MLA_KV_RANK = 256
MLA_NOPE = 64
MLA_ROPE = 32
MLA_V = 64
MLA_QK = MLA_NOPE + MLA_ROPE
GQA_HEADS = 8
GQA_KV_HEADS = 2
GQA_DIM = 64
EVEN_SPLITS = (MLA_Q_RANK, MLA_KV_RANK, MLA_ROPE, GQA_HEADS * GQA_DIM,
               GQA_KV_HEADS * GQA_DIM, GQA_KV_HEADS * GQA_DIM)
EVEN_IN = sum(EVEN_SPLITS)
EVEN_MIX = MLA_HEADS * MLA_V + GQA_HEADS * GQA_DIM
MLSTM_HEADS = 8
MLSTM_QK = 64
MLSTM_V = 128
MLSTM_CONV = 3
MLSTM_CHUNK = 64
ODD_SPLITS = (2 * MLSTM_HEADS * MLSTM_QK, MLSTM_HEADS * MLSTM_V, MLSTM_HEADS * MLSTM_V, 4 * MLSTM_HEADS)
ODD_IN = sum(ODD_SPLITS)
D_FF = 2816
FFN_CONV = 3

Q_BLOCK = 128
ROPE_THETA = 10000.0
EPS = 1e-6

kernel_name = "hybrid_mla_gqa_mlstm_convffn_dit"


def rmsnorm(x, g):
    xf = x.astype(jnp.float32)
    y = xf * lax.rsqrt(jnp.mean(xf * xf, axis=-1, keepdims=True) + EPS)
    return (y * g.astype(jnp.float32)).astype(x.dtype)


def modulate(h, shift, scale):
    return h * (1 + scale) + shift


def split_cols(p, sizes):
    return jnp.split(p, np.cumsum(sizes)[:-1].tolist(), axis=-1)


def dwconv(x, w, b):
    width, T = w.shape[0], x.shape[1]
    left = (width - 1) // 2
    xp = jnp.pad(x, ((0, 0), (left, width - 1 - left), (0, 0)))
    return sum(xp[:, j:j + T] * w[j] for j in range(width)) + b


def axial_rope_tables(n_tokens, rot_dim):
    rows = n_tokens // GRID_W
    row = jnp.repeat(jnp.arange(rows), GRID_W).astype(jnp.float32)
    col = jnp.tile(jnp.arange(GRID_W), rows).astype(jnp.float32)
    n_freq = rot_dim // 4
    inv = ROPE_THETA ** (-jnp.arange(n_freq, dtype=jnp.float32) / n_freq)
    a_r, a_c = row[:, None] * inv, col[:, None] * inv
    ang = jnp.concatenate([a_r, a_r, a_c, a_c], axis=-1)
    return jnp.cos(ang), jnp.sin(ang)


def apply_rope(x, cos, sin):
    a1, a2, b1, b2 = jnp.split(x, 4, axis=-1)
    rot = jnp.concatenate([-a2, a1, -b2, b1], axis=-1)
    shape = (1, x.shape[1]) + (1,) * (x.ndim - 3) + (x.shape[-1],)
    return x * cos.reshape(shape).astype(x.dtype) + rot * sin.reshape(shape).astype(x.dtype)


def attend(q, k, v):
    B, T, G, R, dk = q.shape
    nb = T // Q_BLOCK
    scale = dk ** -0.5
    qb = jnp.moveaxis(q.reshape(B, nb, Q_BLOCK, G, R, dk), 1, 0)

    def block(qi):
        s = jnp.einsum('bqgrd,bsgd->bgrqs', qi, k).astype(jnp.float32) * scale
        p = jax.nn.softmax(s, axis=-1).astype(v.dtype)
        return jnp.einsum('bgrqs,bsgd->bqgrd', p, v)

    out = lax.map(block, qb)
    return jnp.moveaxis(out, 0, 1).reshape(B, T, G, R, v.shape[-1])


def mla_q(cq, g_qa, w_qb, g_q, rope):
    B, T, _ = cq.shape
    q = jnp.dot(rmsnorm(cq, g_qa), w_qb).reshape(B, T, MLA_HEADS, MLA_QK)
    q = rmsnorm(q, g_q)
    if rope is not None:
        q = jnp.concatenate([q[..., :MLA_NOPE], apply_rope(q[..., MLA_NOPE:], *rope)], axis=-1)
    return q[:, :, :, None, :]


def mla_kv(ckv, k_rope, g_kva, w_kvb, g_k, rope):
    B, T, _ = ckv.shape
    kv = jnp.dot(rmsnorm(ckv, g_kva), w_kvb).reshape(B, T, MLA_HEADS, MLA_NOPE + MLA_V)
    k_nope, v = kv[..., :MLA_NOPE], kv[..., MLA_NOPE:]
    k_pe = jnp.broadcast_to(k_rope[:, :, None, :], (B, T, MLA_HEADS, MLA_ROPE))
    k = rmsnorm(jnp.concatenate([k_nope, k_pe], axis=-1), g_k)
    if rope is not None:
        k = jnp.concatenate([k[..., :MLA_NOPE], apply_rope(k[..., MLA_NOPE:], *rope)], axis=-1)
    return k, v


def gqa_q(q, g_q, rope):
    B, T, _ = q.shape
    q = rmsnorm(q.reshape(B, T, GQA_HEADS, GQA_DIM), g_q)
    if rope is not None:
        q = apply_rope(q, *rope)
    return q.reshape(B, T, GQA_KV_HEADS, GQA_HEADS // GQA_KV_HEADS, GQA_DIM)


def gqa_kv(k, v, g_k, rope):
    B, T, _ = k.shape
    k = rmsnorm(k.reshape(B, T, GQA_KV_HEADS, GQA_DIM), g_k)
    if rope is not None:
        k = apply_rope(k, *rope)
    return k, v.reshape(B, T, GQA_KV_HEADS, GQA_DIM)


def even_mixer(h_lat, h_ctx, w_in, g_qa, w_qb, g_kva, w_kvb, g_mq, g_mk, g_gq, g_gk, w_out,
               rope_mla, rope_gqa, need_ctx_out):
    cq_l, ckv_l, kr_l, gq_l, gk_l, gv_l = split_cols(jnp.dot(h_lat, w_in), EVEN_SPLITS)
    cq_c, ckv_c, kr_c, gq_c, gk_c, gv_c = split_cols(jnp.dot(h_ctx, w_in), EVEN_SPLITS)
    ka_c, va_c = mla_kv(ckv_c, kr_c, g_kva, w_kvb, g_mk, None)
    ka_l, va_l = mla_kv(ckv_l, kr_l, g_kva, w_kvb, g_mk, rope_mla)
    kb_c, vb_c = gqa_kv(gk_c, gv_c, g_gk, None)
    kb_l, vb_l = gqa_kv(gk_l, gv_l, g_gk, rope_gqa)
    ka, va = jnp.concatenate([ka_c, ka_l], axis=1), jnp.concatenate([va_c, va_l], axis=1)
    kb, vb = jnp.concatenate([kb_c, kb_l], axis=1), jnp.concatenate([vb_c, vb_l], axis=1)

    def merge(o_a, o_b):
        B, T = o_a.shape[:2]
        return jnp.dot(jnp.concatenate([o_a.reshape(B, T, -1), o_b.reshape(B, T, -1)], axis=-1), w_out)

    out_lat = merge(attend(mla_q(cq_l, g_qa, w_qb, g_mq, rope_mla), ka, va),
                    attend(gqa_q(gq_l, g_gq, rope_gqa), kb, vb))
    out_ctx = None
    if need_ctx_out:
        out_ctx = merge(attend(mla_q(cq_c, g_qa, w_qb, g_mq, None), ka_c, va_c),
                        attend(gqa_q(gq_c, g_gq, None), kb_c, vb_c))
    return out_lat, out_ctx


def mlstm_inputs(h, w_in, conv_w, conv_b, gate_b):
    B, T, _ = h.shape
    qk, v, o, g = split_cols(jnp.dot(h, w_in), ODD_SPLITS)
    q, k = jnp.split(jax.nn.silu(dwconv(qk, conv_w, conv_b)), 2, axis=-1)

    def heads(t, d):
        return jnp.swapaxes(t.reshape(B, T, MLSTM_HEADS, d), 1, 2).astype(jnp.float32)

    g = (g + gate_b).astype(jnp.float32).reshape(B, T, 4, MLSTM_HEADS)
    g = jnp.transpose(g, (2, 0, 3, 1))
    gates = (g[0], jax.nn.log_sigmoid(g[1]), g[2], jax.nn.log_sigmoid(g[3]))
    return heads(q, MLSTM_QK), heads(k, MLSTM_QK) * MLSTM_QK ** -0.5, heads(v, MLSTM_V), o, gates


def mlstm_state_update(state, k, v, log_i, b):
    C, n, m = state
    b_last = b[..., -1]
    a = b_last[..., None] - b + log_i
    m_new = jnp.maximum(b_last + m, jnp.max(a, axis=-1))
    decay = jnp.exp(b_last + m - m_new)
    w = jnp.exp(a - m_new[..., None])
    C_new = decay[..., None, None] * C + jnp.einsum('bhs,bhsv,bhsd->bhvd', w, v, k)
    n_new = decay[..., None] * n + jnp.einsum('bhs,bhsd->bhd', w, k)
    return (C_new, n_new, m_new)


def mlstm_chunkwise(q, k, v, log_i, log_f, state0):
    B, H, T, dk = q.shape
    dv = v.shape[-1]
    L = MLSTM_CHUNK
    nc = T // L
    lower = jnp.tril(jnp.ones((L, L), dtype=bool))

    def chunks(t):
        return jnp.moveaxis(t.reshape((B, H, nc, L) + t.shape[3:]), 2, 0)

    def step(state, inp):
        C, n, m = state
        qc, kc, vc, ic, fc = inp
        b = jnp.cumsum(fc, axis=-1)
        logw = jnp.where(lower, b[..., :, None] - b[..., None, :] + ic[..., None, :], -jnp.inf)
        log_inter = b + m[..., None]
        m_t = jnp.maximum(log_inter, jnp.max(logw, axis=-1))
        w_inter = jnp.exp(log_inter - m_t)
        s = jnp.einsum('bhtd,bhsd->bhts', qc, kc) * jnp.exp(logw - m_t[..., None])
        num = w_inter[..., None] * jnp.einsum('bhvd,bhtd->bhtv', C, qc) + jnp.einsum('bhts,bhsv->bhtv', s, vc)
        den = w_inter * jnp.einsum('bhd,bhtd->bht', n, qc) + jnp.sum(s, axis=-1)
        h = num / jnp.maximum(jnp.abs(den), jnp.exp(-m_t))[..., None]
        return mlstm_state_update(state, kc, vc, ic, b), h

    state, h = lax.scan(step, state0, (chunks(q), chunks(k), chunks(v), chunks(log_i), chunks(log_f)))
    return jnp.moveaxis(h, 0, 2).reshape(B, H, T, dv), state


def mlstm_direction(q_l, k_l, v_l, li_l, lf_l, q_c, k_c, v_c, li_c, lf_c, need_ctx_out):
    B, H = q_l.shape[:2]
    state0 = (jnp.zeros((B, H, MLSTM_V, MLSTM_QK), jnp.float32),
              jnp.zeros((B, H, MLSTM_QK), jnp.float32),
              jnp.zeros((B, H), jnp.float32))
    if need_ctx_out:
        h_c, state_c = mlstm_chunkwise(q_c, k_c, v_c, li_c, lf_c, state0)
    else:
        h_c = None
        state_c = mlstm_state_update(state0, k_c, v_c, li_c, jnp.cumsum(lf_c, axis=-1))
    h_l, _ = mlstm_chunkwise(q_l, k_l, v_l, li_l, lf_l, state_c)
    return h_l, h_c


def odd_mixer(h_lat, h_ctx, w_in, conv_w, conv_b, gate_b, out_g, w_out, need_ctx_out):
    q_l, k_l, v_l, o_l, g_l = mlstm_inputs(h_lat, w_in, conv_w, conv_b, gate_b)
    q_c, k_c, v_c, o_c, g_c = mlstm_inputs(h_ctx, w_in, conv_w, conv_b, gate_b)

    def rev(t):
        return jnp.flip(t, axis=2)

    hf_l, hf_c = mlstm_direction(q_l, k_l, v_l, g_l[0], g_l[1], q_c, k_c, v_c, g_c[0], g_c[1], need_ctx_out)
    hb_l, hb_c = mlstm_direction(rev(q_l), rev(k_l), rev(v_l), rev(g_l[2]), rev(g_l[3]),
                                 rev(q_c), rev(k_c), rev(v_c), rev(g_c[2]), rev(g_c[3]), need_ctx_out)

    def readout(hf, hb, o):
        h = jnp.swapaxes(hf + rev(hb), 1, 2)
        B, T = h.shape[:2]
        h = rmsnorm(h, out_g).reshape(B, T, -1).astype(o.dtype)
        return jnp.dot(h * jax.nn.sigmoid(o), w_out)

    out_lat = readout(hf_l, hb_l, o_l)
    out_ctx = readout(hf_c, hb_c, o_c) if need_ctx_out else None
    return out_lat, out_ctx


def conv_ffn(h, w_up, conv_w, conv_b, w_down):
    gate, val = jnp.split(jnp.dot(h, w_up), 2, axis=-1)
    return jnp.dot(jax.nn.silu(dwconv(gate, conv_w, conv_b)) * val, w_down)


def setup_inputs(seed: int = 0) -> dict:
    key = jax.random.key(seed)
    keys = jax.random.split(key, 40)
    counter = iter(range(40))
    D = D_MODEL

    def nrm(shape, scale):
        return jax.random.normal(keys[next(counter)], shape, jnp.float32) * scale

    def gain(shape):
        return 1.0 + nrm(shape, 0.05)

    H = MLSTM_HEADS
    i_bias = nrm((N_ODD, 2, 1, H), 0.1)
    f_bias = jnp.linspace(3.0, 6.0, H, dtype=jnp.float32) + nrm((N_ODD, 2, 1, H), 0.1)
    ml_gate_b = jnp.concatenate([i_bias, f_bias], axis=2).reshape(N_ODD, 4 * H)
    return {
        'x': nrm((BATCH, SEQ, D), 1.0),
        'c': nrm((BATCH, D), 1.0),
        'ctx': nrm((BATCH, CTX_LEN, D), 1.0),
        'c_ctx': nrm((D,), 1.0),
        'ada_w': nrm((DEPTH, D, 6 * D), 0.5 * D ** -0.5),
        'ada_b': nrm((DEPTH, 6 * D), 0.02),
        'norm1_g': gain((DEPTH, D)),
        'norm2_g': gain((DEPTH, D)),
        'ffn_w_up': nrm((DEPTH, D, 2 * D_FF), D ** -0.5),
        'ffn_conv_w': nrm((DEPTH, FFN_CONV, D_FF), 0.5),
        'ffn_conv_b': nrm((DEPTH, D_FF), 0.02),
        'ffn_w_down': nrm((DEPTH, D_FF, D), D_FF ** -0.5),
        'att_w_in': nrm((N_EVEN, D, EVEN_IN), D ** -0.5),
        'mla_qa_g': gain((N_EVEN, MLA_Q_RANK)),
        'mla_w_qb': nrm((N_EVEN, MLA_Q_RANK, MLA_HEADS * MLA_QK), MLA_Q_RANK ** -0.5),
        'mla_kva_g': gain((N_EVEN, MLA_KV_RANK)),
        'mla_w_kvb': nrm((N_EVEN, MLA_KV_RANK, MLA_HEADS * (MLA_NOPE + MLA_V)), MLA_KV_RANK ** -0.5),
        'mla_q_g': gain((N_EVEN, MLA_QK)),
        'mla_k_g': gain((N_EVEN, MLA_QK)),
        'gqa_q_g': gain((N_EVEN, GQA_DIM)),
        'gqa_k_g': gain((N_EVEN, GQA_DIM)),
        'att_w_out': nrm((N_EVEN, EVEN_MIX, D), EVEN_MIX ** -0.5),
        'ml_w_in': nrm((N_ODD, D, ODD_IN), D ** -0.5),
        'ml_conv_w': nrm((N_ODD, MLSTM_CONV, 2 * H * MLSTM_QK), 0.5),
        'ml_conv_b': nrm((N_ODD, 2 * H * MLSTM_QK), 0.02),
        'ml_gate_b': ml_gate_b,
        'ml_out_g': gain((N_ODD, H, MLSTM_V)),
        'ml_w_out': nrm((N_ODD, H * MLSTM_V, D), (H * MLSTM_V) ** -0.5),
    }


def reference(x, c, ctx, c_ctx, ada_w, ada_b, norm1_g, norm2_g, ffn_w_up, ffn_conv_w, ffn_conv_b,
              ffn_w_down, att_w_in, mla_qa_g, mla_w_qb, mla_kva_g, mla_w_kvb, mla_q_g, mla_k_g,
              gqa_q_g, gqa_k_g, att_w_out, ml_w_in, ml_conv_w, ml_conv_b, ml_gate_b, ml_out_g, ml_w_out):
    n_lat = x.shape[1]
    rope_mla = axial_rope_tables(n_lat, MLA_ROPE)
    rope_gqa = axial_rope_tables(n_lat, GQA_DIM)
    for layer in range(DEPTH):
        need_ctx_out = layer < DEPTH - 1
        j = layer // 2
        mod_lat = (jnp.dot(jax.nn.silu(c), ada_w[layer]) + ada_b[layer])[:, None, :]
        mod_ctx = jnp.dot(jax.nn.silu(c_ctx), ada_w[layer]) + ada_b[layer]
        sh1, sc1, gt1, sh2, sc2, gt2 = jnp.split(mod_lat, 6, axis=-1)
        csh1, csc1, cgt1, csh2, csc2, cgt2 = jnp.split(mod_ctx, 6, axis=-1)
        h_lat = modulate(rmsnorm(x, norm1_g[layer]), sh1, sc1)
        h_ctx = modulate(rmsnorm(ctx, norm1_g[layer]), csh1, csc1)
        if layer % 2 == 0:
            out_lat, out_ctx = even_mixer(h_lat, h_ctx, att_w_in[j], mla_qa_g[j], mla_w_qb[j], mla_kva_g[j],
                                          mla_w_kvb[j], mla_q_g[j], mla_k_g[j], gqa_q_g[j], gqa_k_g[j],
                                          att_w_out[j], rope_mla, rope_gqa, need_ctx_out)
        else:
            out_lat, out_ctx = odd_mixer(h_lat, h_ctx, ml_w_in[j], ml_conv_w[j], ml_conv_b[j], ml_gate_b[j],
                                         ml_out_g[j], ml_w_out[j], need_ctx_out)
        x = x + gt1 * out_lat
        x = x + gt2 * conv_ffn(modulate(rmsnorm(x, norm2_g[layer]), sh2, sc2),
                               ffn_w_up[layer], ffn_conv_w[layer], ffn_conv_b[layer], ffn_w_down[layer])
        if need_ctx_out:
            ctx = ctx + cgt1 * out_ctx
            ctx = ctx + cgt2 * conv_ffn(modulate(rmsnorm(ctx, norm2_g[layer]), csh2, csc2),
                                       ffn_w_up[layer], ffn_conv_w[layer], ffn_conv_b[layer], ffn_w_down[layer])
    return x
```

```python
import functools

import numpy as np
import jax
import jax.numpy as jnp
from jax import lax
from jax.experimental import pallas as pl
from jax.experimental.pallas import tpu as pltpu

F32 = jnp.float32
BF16 = jnp.bfloat16

EPS = 1e-6
ROPE_THETA = 10000.0
GRID_W = 64
LOG2E = 1.4426950408889634

MLA_HEADS = 8
MLA_Q_RANK = 384
MLA_KV_RANK = 256
MLA_NOPE = 64
MLA_ROPE = 32
MLA_V = 64
MLA_QK = MLA_NOPE + MLA_ROPE
GQA_HEADS = 8
GQA_KV_HEADS = 2
GQA_DIM = 64
MLSTM_HEADS = 8
MLSTM_QK = 64
MLSTM_V = 128
D_FF = 2816

LANES = 128
SUBLANES = 8
VMEM_LIMIT = 56 * 1024 * 1024

TM_IN = 256
TM_OUT = 512
TM_FFN = 512
FF_CHUNK = 256
TQ = 256
KV_CHUNK = 256
ML_CHUNK = 128


def _cparams(sem):
    return pltpu.CompilerParams(dimension_semantics=sem, vmem_limit_bytes=VMEM_LIMIT)


def _dot(a, b):
    return jnp.dot(a, b, preferred_element_type=F32)


def _dot_nt(a, b):
    return lax.dot_general(a, b, (((1,), (1,)), ((), ())), preferred_element_type=F32)


def _dot_tn(a, b):
    return lax.dot_general(a, b, (((0,), (0,)), ((), ())), preferred_element_type=F32)


def _rms(x, gain_row, inv_n):
    ss = jnp.sum(x * x, axis=-1, keepdims=True) * inv_n
    return x * lax.rsqrt(ss + EPS) * gain_row


def _norm_mod(x, g_row, sh_row, sc_row):
    ms = jnp.mean(x * x, axis=-1, keepdims=True)
    return x * lax.rsqrt(ms + EPS) * (g_row * (1.0 + sc_row)) + sh_row


def _silu(x):
    return x * (1.0 / (1.0 + jnp.exp(-x)))


def _adaln_kernel(c_ref, w_ref, b_ref, o_ref):
    c = _silu(c_ref[...]).astype(BF16)
    o_ref[...] = _dot(c, w_ref[...].astype(BF16)) + b_ref[...]


def _adaln(cvec, ada_w, ada_b):
    depth, d, n = ada_w.shape
    rows = cvec.shape[0]
    tn = 1536
    return pl.pallas_call(
        _adaln_kernel,
        out_shape=jax.ShapeDtypeStruct((depth, rows, n), F32),
        grid=(depth, n // tn),
        in_specs=[
            pl.BlockSpec((rows, d), lambda l, j: (0, 0)),
            pl.BlockSpec((None, d, tn), lambda l, j: (l, 0, j)),
            pl.BlockSpec((None, 1, tn), lambda l, j: (l, 0, j)),
        ],
        out_specs=pl.BlockSpec((None, rows, tn), lambda l, j: (l, 0, j)),
        compiler_params=_cparams(("arbitrary", "arbitrary")),
        name="adaln",
    )(cvec, ada_w, ada_b.reshape(depth, 1, n))


def _rope(y, cos, sa, sb, shift):
    return y * cos + pltpu.roll(y, LANES - shift, 1) * sa + pltpu.roll(y, shift, 1) * sb


def _even_in_kernel(x_ref, sh_ref, sc_ref, g1_ref, win_ref, gqa_ref, wqb_ref, gkva_ref, wk_ref, wv_ref,
                    gmq_ref, gmk_ref, ggq_ref, ggk_ref, rm_ref, rg_ref,
                    qm_ref, km_ref, vm_ref, qg_ref, kg_ref, vg_ref):
    h = _norm_mod(x_ref[...], g1_ref[...], sh_ref[...], sc_ref[...]).astype(BF16)
    cos_m, sa_m, sb_m = rm_ref[0], rm_ref[1], rm_ref[2]
    cos_g, sa_g, sb_g = rg_ref[0], rg_ref[1], rg_ref[2]
    o_ckv = MLA_Q_RANK
    o_kr = o_ckv + MLA_KV_RANK
    o_gq = o_kr + LANES
    o_gk = o_gq + GQA_HEADS * LANES
    o_gv = o_gk + GQA_KV_HEADS * LANES

    cq = _dot(h, win_ref[:, 0:o_ckv])
    cqn = _rms(cq, gqa_ref[...], 1.0 / MLA_Q_RANK).astype(BF16)
    for hd in range(MLA_HEADS):
        blk = _dot(cqn, wqb_ref[:, hd * LANES:(hd + 1) * LANES])
        y = _rms(blk, gmq_ref[...], 1.0 / MLA_QK)
        qm_ref[:, hd * LANES:(hd + 1) * LANES] = _rope(y, cos_m, sa_m, sb_m, MLA_ROPE // 4).astype(BF16)

    ckv = _dot(h, win_ref[:, o_ckv:o_kr])
    ckvn = _rms(ckv, gkva_ref[...], 1.0 / MLA_KV_RANK).astype(BF16)
    krp = _dot(h, win_ref[:, o_kr:o_gq])
    for hd in range(MLA_HEADS):
        blk = _dot(ckvn, wk_ref[:, hd * LANES:(hd + 1) * LANES]) + krp
        y = _rms(blk, gmk_ref[...], 1.0 / MLA_QK)
        km_ref[:, hd * LANES:(hd + 1) * LANES] = _rope(y, cos_m, sa_m, sb_m, MLA_ROPE // 4).astype(BF16)
    vm_ref[...] = _dot(ckvn, wv_ref[...]).astype(BF16)

    for hd in range(GQA_HEADS):
        blk = _dot(h, win_ref[:, o_gq + hd * LANES:o_gq + (hd + 1) * LANES])
        y = _rms(blk, ggq_ref[...], 1.0 / GQA_DIM)
        qg_ref[:, hd * LANES:(hd + 1) * LANES] = _rope(y, cos_g, sa_g, sb_g, GQA_DIM // 4).astype(BF16)
    for hd in range(GQA_KV_HEADS):
        blk = _dot(h, win_ref[:, o_gk + hd * LANES:o_gk + (hd + 1) * LANES])
        y = _rms(blk, ggk_ref[...], 1.0 / GQA_DIM)
        kg_ref[:, hd * LANES:(hd + 1) * LANES] = _rope(y, cos_g, sa_g, sb_g, GQA_DIM // 4).astype(BF16)
    vg_ref[...] = _dot(h, win_ref[:, o_gv:o_gv + GQA_KV_HEADS * LANES]).astype(BF16)


def _pad_cols(w, width):
    return jnp.pad(w, ((0, 0), (0, width - w.shape[1])))


def _even_weights(w_in, w_qb, w_kvb, g_mq, g_mk, g_gq, g_gk):
    d = w_in.shape[0]
    s0 = MLA_Q_RANK
    s1 = s0 + MLA_KV_RANK
    s2 = s1 + MLA_ROPE
    s3 = s2 + GQA_HEADS * GQA_DIM
    s4 = s3 + GQA_KV_HEADS * GQA_DIM
    cq, ckv, kr = w_in[:, :s0], w_in[:, s0:s1], w_in[:, s1:s2]
    gq, gk, gv = w_in[:, s2:s3], w_in[:, s3:s4], w_in[:, s4:]
    kr_p = jnp.pad(kr, ((0, 0), (MLA_NOPE, LANES - MLA_QK)))
    gq_p = jnp.pad(gq.reshape(d, GQA_HEADS, GQA_DIM), ((0, 0), (0, 0), (0, LANES - GQA_DIM))).reshape(d, -1)
    gk_p = jnp.pad(gk.reshape(d, GQA_KV_HEADS, GQA_DIM), ((0, 0), (0, 0), (0, LANES - GQA_DIM))).reshape(d, -1)
    gv3 = gv.reshape(d, GQA_KV_HEADS, GQA_DIM)
    gv_p = jnp.concatenate([gv3, gv3], axis=-1).reshape(d, -1)
    win = jnp.concatenate([cq, ckv, kr_p, gq_p, gk_p, gv_p], axis=1).astype(BF16)
    wqb = jnp.pad(w_qb.reshape(MLA_Q_RANK, MLA_HEADS, MLA_QK),
                  ((0, 0), (0, 0), (0, LANES - MLA_QK))).reshape(MLA_Q_RANK, -1).astype(BF16)
    kv3 = w_kvb.reshape(MLA_KV_RANK, MLA_HEADS, MLA_NOPE + MLA_V)
    wk = jnp.pad(kv3[:, :, :MLA_NOPE], ((0, 0), (0, 0), (0, LANES - MLA_NOPE))).reshape(MLA_KV_RANK, -1).astype(BF16)
    wv = kv3[:, :, MLA_NOPE:].reshape(MLA_KV_RANK, -1).astype(BF16)
    gmq = _pad_cols((g_mq * (MLA_QK ** -0.5 * LOG2E))[None, :], LANES)
    gmk = _pad_cols(g_mk[None, :], LANES)
    ggq = _pad_cols((g_gq * (GQA_DIM ** -0.5 * LOG2E))[None, :], LANES)
    ggk = _pad_cols(g_gk[None, :], LANES)
    return win, wqb, wk, wv, gmq, gmk, ggq, ggk


def _rope_tables(n_tokens, rot_dim, offset):
    rows = n_tokens // GRID_W
    row = jnp.repeat(jnp.arange(rows), GRID_W).astype(F32)
    col = jnp.tile(jnp.arange(GRID_W), rows).astype(F32)
    n_freq = rot_dim // 4
    inv = ROPE_THETA ** (-jnp.arange(n_freq, dtype=F32) / n_freq)
    a_r, a_c = row[:, None] * inv, col[:, None] * inv
    ang = jnp.concatenate([a_r, a_r, a_c, a_c], axis=-1)
    cos, sin = jnp.cos(ang), jnp.sin(ang)
    quarter = (np.arange(rot_dim) // n_freq) % 2
    sa = jnp.where(quarter == 0, -sin, 0.0)
    sb = jnp.where(quarter == 1, sin, 0.0)
    pad = ((0, 0), (offset, LANES - offset - rot_dim))
    lat = jnp.stack([jnp.pad(cos, pad, constant_values=1.0), jnp.pad(sa, pad), jnp.pad(sb, pad)])
    ident = jnp.stack([jnp.ones((n_tokens, LANES), F32), jnp.zeros((n_tokens, LANES), F32),
                       jnp.zeros((n_tokens, LANES), F32)])
    return jnp.stack([lat, ident])


def _even_in(X, sh, sc, g1, ew, gqa, gkva, rope_m, rope_g):
    nb, t, d = X.shape
    win, wqb, wk, wv, gmq, gmk, ggq, ggk = ew
    tm = TM_IN
    full = lambda a: pl.BlockSpec(a.shape, lambda b, i: (0,) * a.ndim)
    row = pl.BlockSpec((None, 1, d), lambda b, i: (b, 0, 0))
    rope_spec = pl.BlockSpec((None, 3, tm, LANES), lambda b, i: (jnp.where(b == nb - 1, 1, 0), 0, i, 0))

    def out(width):
        return jax.ShapeDtypeStruct((nb, t, width), BF16), pl.BlockSpec((None, tm, width), lambda b, i: (b, i, 0))

    outs = [out(MLA_HEADS * LANES), out(MLA_HEADS * LANES), out(MLA_HEADS * MLA_V),
            out(GQA_HEADS * LANES), out(GQA_KV_HEADS * LANES), out(GQA_KV_HEADS * LANES)]
    return pl.pallas_call(
        _even_in_kernel,
        out_shape=[o[0] for o in outs],
        grid=(nb, t // tm),
        in_specs=[pl.BlockSpec((None, tm, d), lambda b, i: (b, i, 0)), row, row, full(g1), full(win), full(gqa),
                  full(wqb), full(gkva), full(wk), full(wv), full(gmq), full(gmk), full(ggq), full(ggk),
                  rope_spec, rope_spec],
        out_specs=[o[1] for o in outs],
        compiler_params=_cparams(("parallel", "parallel")),
        name="even_in",
    )(X, sh, sc, g1, win, gqa, wqb, gkva, wk, wv, gmq, gmk, ggq, ggk, rope_m, rope_g)


def _attn_kernel(q_ref, kl_ref, kc_ref, vl_ref, vc_ref, o_ref, s_scr, *, n_heads, shared_k, n_lat_batches):
    tq = q_ref.shape[0]
    lane = lax.broadcasted_iota(jnp.int32, (tq, LANES), 1)

    def run(parts):
        outs = []
        for hd in range(n_heads):
            q = q_ref[:, hd * LANES:(hd + 1) * LANES]
            kc0 = 0 if shared_k else hd * LANES
            mx = jnp.full((tq, LANES), -jnp.inf, F32)
            off = 0
            for k_ref, _, n_keys in parts:
                for c in range(n_keys // KV_CHUNK):
                    kb = k_ref[c * KV_CHUNK:(c + 1) * KV_CHUNK, kc0:kc0 + LANES]
                    s = _dot_nt(q, kb)
                    s_scr[:, off:off + KV_CHUNK] = s
                    for j in range(KV_CHUNK // LANES):
                        mx = jnp.maximum(mx, s[:, j * LANES:(j + 1) * LANES])
                    off += KV_CHUNK
            m = jnp.max(mx, axis=-1, keepdims=True)
            lsum = jnp.zeros((tq, LANES), F32)
            acc = jnp.zeros((tq, LANES), F32)
            off = 0
            for _, v_ref, n_keys in parts:
                for c in range(n_keys // KV_CHUNK):
                    p = jnp.exp2(s_scr[:, off:off + KV_CHUNK] - m)
                    for j in range(KV_CHUNK // LANES):
                        lsum = lsum + p[:, j * LANES:(j + 1) * LANES]
                    acc = acc + _dot(p.astype(BF16), v_ref[c * KV_CHUNK:(c + 1) * KV_CHUNK, :])
                    off += KV_CHUNK
            outs.append(acc * (1.0 / jnp.sum(lsum, axis=-1, keepdims=True)))
        for j in range(n_heads // 2):
            pair = jnp.where(lane < LANES // 2, outs[2 * j], outs[2 * j + 1])
            o_ref[:, j * LANES:(j + 1) * LANES] = pair.astype(o_ref.dtype)

    n_ctx = kc_ref.shape[0]
    n_lat = kl_ref.shape[0]
    is_lat = pl.program_id(0) < n_lat_batches

    @pl.when(is_lat)
    def _():
        run([(kc_ref, vc_ref, n_ctx), (kl_ref, vl_ref, n_lat)])

    @pl.when(jnp.logical_not(is_lat))
    def _():
        run([(kc_ref, vc_ref, n_ctx)])


def _attention(Q, K, V, *, n_heads, shared_k, ctx_len):
    nb, t, _ = Q.shape
    nbl = nb - 1
    assert ctx_len == TQ and nbl * ctx_len == t
    nk = 1 if shared_k else n_heads
    groups = V.shape[-1] // LANES
    qw, kw, ow = n_heads * LANES, nk * LANES, n_heads * LANES // 2
    lat_b = lambda b: jnp.minimum(b, nbl - 1)
    ctx_blk = lambda b, i: jnp.where(b < nbl, b, i)
    return pl.pallas_call(
        functools.partial(_attn_kernel, n_heads=n_heads, shared_k=shared_k, n_lat_batches=nbl),
        out_shape=jax.ShapeDtypeStruct((nb, t, groups * ow), BF16),
        grid=(nb, groups, t // TQ),
        in_specs=[
            pl.BlockSpec((None, TQ, qw), lambda b, g, i: (b, i, g)),
            pl.BlockSpec((None, t, kw), lambda b, g, i: (lat_b(b), 0, g)),
            pl.BlockSpec((None, ctx_len, kw), lambda b, g, i: (nbl, ctx_blk(b, i), g)),
            pl.BlockSpec((None, t, LANES), lambda b, g, i: (lat_b(b), 0, g)),
            pl.BlockSpec((None, ctx_len, LANES), lambda b, g, i: (nbl, ctx_blk(b, i), g)),
        ],
        out_specs=pl.BlockSpec((None, TQ, ow), lambda b, g, i: (b, i, g)),
        scratch_shapes=[pltpu.VMEM((TQ, t + ctx_len), F32)],
        compiler_params=_cparams(("parallel", "parallel", "arbitrary")),
        name="attn_shared_k" if shared_k else "attn",
    )(Q, K, K, V, V)


def _even_out_kernel(x_ref, gt_ref, oa_ref, ob_ref, w_ref, y_ref):
    na = oa_ref.shape[1]
    out = _dot(oa_ref[...], w_ref[0:na, :]) + _dot(ob_ref[...], w_ref[na:, :])
    y_ref[...] = x_ref[...] + gt_ref[...] * out


def _even_out(X, gt, Oa, Ob, w_out):
    nb, t, d = X.shape
    tm = TM_OUT
    tile = lambda w: pl.BlockSpec((None, tm, w), lambda b, i: (b, i, 0))
    return pl.pallas_call(
        _even_out_kernel,
        out_shape=jax.ShapeDtypeStruct((nb, t, d), F32),
        grid=(nb, t // tm),
        in_specs=[tile(d), pl.BlockSpec((None, 1, d), lambda b, i: (b, 0, 0)), tile(Oa.shape[-1]), tile(Ob.shape[-1]),
                  pl.BlockSpec(w_out.shape, lambda b, i: (0, 0))],
        out_specs=tile(d),
        compiler_params=_cparams(("parallel", "parallel")),
        name="even_out",
    )(X, gt, Oa, Ob, w_out)


def _ffn_kernel(x_ref, xp_ref, xn_ref, sh_ref, sc_ref, gt_ref, g2_ref, wg_ref, wv_ref, cw_ref, cb_ref, wd_ref,
                y_ref, a_scr, *, seq_len, ctx_len, n_lat_batches):
    tm = x_ref.shape[0]
    text = tm + 2 * SUBLANES
    x = x_ref[...]
    g2, sh, sc = g2_ref[...], sh_ref[...], sc_ref[...]
    xext = jnp.concatenate([xp_ref[...], x, xn_ref[...]], axis=0)
    hext = _norm_mod(xext, g2, sh, sc).astype(BF16)
    seg = jnp.where(pl.program_id(0) < n_lat_batches, seq_len, ctx_len)
    pos = (pl.program_id(1) * tm + lax.broadcasted_iota(jnp.int32, (tm, 1), 0)) % seg
    has_prev = pos != 0
    has_next = pos != seg - 1
    mid = slice(SUBLANES, SUBLANES + tm)
    for j in range(D_FF // FF_CHUNK):
        cols = slice(j * FF_CHUNK, (j + 1) * FF_CHUNK)
        gext = _dot(hext, wg_ref[:, cols])
        val = _dot(hext, wv_ref[:, cols])[mid]
        prev = jnp.where(has_prev, pltpu.roll(gext, 1, 0)[mid], 0.0)
        nxt = jnp.where(has_next, pltpu.roll(gext, text - 1, 0)[mid], 0.0)
        conv = prev * cw_ref[0:1, cols] + gext[mid] * cw_ref[1:2, cols] + nxt * cw_ref[2:3, cols] + cb_ref[:, cols]
        a_scr[:, cols] = (_silu(conv) * val).astype(BF16)
    y_ref[...] = x + gt_ref[...] * _dot(a_scr[...], wd_ref[...])


def _ffn(X, sh, sc, gt, g2, wg, wv, cw, cb, wd, *, n_out_batches, n_lat_batches, ctx_len):
    nb, t, d = X.shape
    tm = TM_FFN
    nblk = t // SUBLANES
    per = tm // SUBLANES
    row = pl.BlockSpec((None, 1, d), lambda b, i: (b, 0, 0))
    full = lambda a: pl.BlockSpec(a.shape, lambda b, i: (0,) * a.ndim, pipeline_mode=pl.Buffered(1))
    return pl.pallas_call(
        functools.partial(_ffn_kernel, seq_len=t, ctx_len=ctx_len, n_lat_batches=n_lat_batches),
        out_shape=jax.ShapeDtypeStruct((n_out_batches, t, d), F32),
        grid=(n_out_batches, t // tm),
        in_specs=[
            pl.BlockSpec((None, tm, d), lambda b, i: (b, i, 0)),
            pl.BlockSpec((None, SUBLANES, d), lambda b, i: (b, jnp.maximum(i * per - 1, 0), 0)),
            pl.BlockSpec((None, SUBLANES, d), lambda b, i: (b, jnp.minimum((i + 1) * per, nblk - 1), 0)),
            row, row, row, full(g2), full(wg), full(wv), full(cw), full(cb), full(wd),
        ],
        out_specs=pl.BlockSpec((None, tm, d), lambda b, i: (b, i, 0)),
        scratch_shapes=[pltpu.VMEM((tm, D_FF), BF16)],
        compiler_params=_cparams(("parallel", "parallel")),
        name="conv_ffn",
    )(X, X, X, sh, sc, gt, g2, wg, wv, cw, cb, wd)


def _odd_in_kernel(x_ref, sh_ref, sc_ref, g1_ref, w_ref, gb_ref, qk_ref, v_ref, o_ref, g_ref):
    h = _norm_mod(x_ref[...], g1_ref[...], sh_ref[...], sc_ref[...]).astype(BF16)
    nqk = qk_ref.shape[1]
    nv = v_ref.shape[1]
    qk_ref[...] = _dot(h, w_ref[:, 0:nqk])
    v_ref[...] = _dot(h, w_ref[:, nqk:nqk + nv]).astype(BF16)
    o_ref[...] = _dot(h, w_ref[:, nqk + nv:nqk + 2 * nv])
    g_ref[...] = _dot(h, w_ref[:, nqk + 2 * nv:]) + gb_ref[...]


def _odd_in(X, sh, sc, g1, w, gate_b):
    nb, t, d = X.shape
    tm = TM_OUT
    nqk = 2 * MLSTM_HEADS * MLSTM_QK
    nv = MLSTM_HEADS * MLSTM_V
    row = pl.BlockSpec((None, 1, d), lambda b, i: (b, 0, 0))
    full = lambda a: pl.BlockSpec(a.shape, lambda b, i: (0,) * a.ndim)
    tile = lambda wd: pl.BlockSpec((None, tm, wd), lambda b, i: (b, i, 0))
    return pl.pallas_call(
        _odd_in_kernel,
        out_shape=[jax.ShapeDtypeStruct((nb, t, nqk), F32), jax.ShapeDtypeStruct((nb, t, nv), BF16),
                   jax.ShapeDtypeStruct((nb, t, nv), F32), jax.ShapeDtypeStruct((nb, t, LANES), F32)],
        grid=(nb, t // tm),
        in_specs=[tile(d), row, row, full(g1), full(w), full(gate_b)],
        out_specs=[tile(nqk), tile(nv), tile(nv), tile(LANES)],
        compiler_params=_cparams(("parallel", "parallel")),
        name="odd_in",
    )(X, sh, sc, g1, w, gate_b)


def _qk_conv_kernel(x_ref, cw_ref, cb_ref, y_ref, *, ctx_len, n_lat_batches, n_q_blocks):
    t = x_ref.shape[0]
    x = x_ref[...]
    seg = jnp.where(pl.program_id(0) < n_lat_batches, t, ctx_len)
    pos = lax.broadcasted_iota(jnp.int32, (t, 1), 0) % seg
    prev = jnp.where(pos != 0, pltpu.roll(x, 1, 0), 0.0)
    nxt = jnp.where(pos != seg - 1, pltpu.roll(x, t - 1, 0), 0.0)
    conv = prev * cw_ref[0:1, :] + x * cw_ref[1:2, :] + nxt * cw_ref[2:3, :] + cb_ref[...]
    scale = jnp.where(pl.program_id(1) >= n_q_blocks, MLSTM_QK ** -0.5, 1.0)
    y_ref[...] = (_silu(conv) * scale).astype(BF16)


def _qk_conv(QK, cw, cb, *, ctx_len):
    nb, t, n = QK.shape
    return pl.pallas_call(
        functools.partial(_qk_conv_kernel, ctx_len=ctx_len, n_lat_batches=nb - 1, n_q_blocks=n // (2 * LANES)),
        out_shape=jax.ShapeDtypeStruct((nb, t, n), BF16),
        grid=(nb, n // LANES),
        in_specs=[pl.BlockSpec((None, t, LANES), lambda b, j: (b, 0, j)),
                  pl.BlockSpec((cw.shape[0], LANES), lambda b, j: (0, j)),
                  pl.BlockSpec((1, LANES), lambda b, j: (0, j))],
        out_specs=pl.BlockSpec((None, t, LANES), lambda b, j: (b, 0, j)),
        compiler_params=_cparams(("parallel", "parallel")),
        name="qk_conv",
    )(QK, cw, cb)


def _log_sigmoid(x):
    return jnp.minimum(x, 0.0) - jnp.log(1.0 + jnp.exp(-jnp.abs(x)))


def _mlstm_kernel(q_ref, k_ref, v_ref, g_ref, h_ref, ct_scr, n_scr, m_scr, *, rev):
    @pl.when(pl.program_id(1) == 0)
    def _():
        ct_scr[...] = jnp.zeros_like(ct_scr)
        n_scr[...] = jnp.zeros_like(n_scr)
        m_scr[...] = jnp.zeros_like(m_scr)

    L = q_ref.shape[0]
    g = g_ref[...]
    lf = _log_sigmoid(g)
    r_i = lax.broadcasted_iota(jnp.int32, (L, L), 0)
    c_i = lax.broadcasted_iota(jnp.int32, (L, L), 1)
    tri = (c_i >= r_i) if rev else (c_i <= r_i)
    b_all = jnp.dot(tri.astype(F32), lf, precision=lax.Precision.HIGHEST, preferred_element_type=F32)
    g_t = g.T
    b_t = b_all.T
    lane = lax.broadcasted_iota(jnp.int32, (L, LANES), 1)
    i_off = 2 * MLSTM_HEADS if rev else 0
    f_off = i_off + MLSTM_HEADS
    last = 0 if rev else L - 1
    for hd in range(MLSTM_HEADS):
        pair = hd // 2
        qp = q_ref[:, pair * LANES:(pair + 1) * LANES]
        kp = k_ref[:, pair * LANES:(pair + 1) * LANES]
        own = (lane < MLSTM_QK) if hd % 2 == 0 else (lane >= MLSTM_QK)
        kh = jnp.where(own, kp, jnp.zeros_like(kp))
        vh = v_ref[:, hd * LANES:(hd + 1) * LANES]
        ic = g[:, i_off + hd:i_off + hd + 1]
        bc = b_all[:, f_off + hd:f_off + hd + 1]
        ir = g_t[i_off + hd:i_off + hd + 1, :]
        br = b_t[f_off + hd:f_off + hd + 1, :]
        b_last = bc[last:last + 1, :]
        m_prev = m_scr[hd][:, 0:1]
        ct = ct_scr[hd]
        n_prev = n_scr[hd]

        logw = jnp.where(tri, bc - br + ir, -jnp.inf)
        log_inter = bc + m_prev
        m_t = jnp.maximum(log_inter, jnp.max(logw, axis=1, keepdims=True))
        w_inter = jnp.exp(log_inter - m_t)
        s = _dot_nt(qp, kh) * jnp.exp(logw - m_t)
        num = w_inter * _dot(qp, ct.astype(BF16)) + _dot(s.astype(BF16), vh)
        qn = jnp.sum(qp.astype(F32) * n_prev, axis=1, keepdims=True)
        den = w_inter * qn + jnp.sum(s, axis=1, keepdims=True)
        h_ref[:, hd * LANES:(hd + 1) * LANES] = num / jnp.maximum(jnp.abs(den), jnp.exp(-m_t))

        a = b_last - bc + ic
        m_new = jnp.maximum(b_last + m_prev, jnp.max(a, axis=0, keepdims=True))
        decay = jnp.exp(b_last + m_prev - m_new)
        w = jnp.exp(a - m_new)
        vw = (vh.astype(F32) * w).astype(BF16)
        ct_scr[hd] = decay * ct + _dot_tn(kh, vw)
        n_scr[hd] = decay * n_prev + jnp.sum(kh.astype(F32) * w, axis=0, keepdims=True)
        m_scr[hd] = jnp.broadcast_to(m_new, (1, LANES))


def _mlstm(QK, V, G, *, rev, ctx_len):
    nb, t, nqk = QK.shape
    nbl = nb - 1
    L = ML_CHUNK
    ncc, nlc = ctx_len // L, t // L
    nq = nqk // 2

    def blk(b, c):
        if rev:
            cc, lc = ncc - 1 - c, nlc - 1 - (c - ncc)
        else:
            cc, lc = c, c - ncc
        is_ctx = c < ncc
        return jnp.where(is_ctx, nbl, b), jnp.where(is_ctx, b * ncc + cc, lc)

    def spec(width, col):
        return pl.BlockSpec((None, L, width), lambda b, c: (*blk(b, c), col))

    return pl.pallas_call(
        functools.partial(_mlstm_kernel, rev=rev),
        out_shape=jax.ShapeDtypeStruct((nb, t, V.shape[-1]), F32),
        grid=(nbl, ncc + nlc),
        in_specs=[spec(nq, 0), spec(nq, 1), spec(V.shape[-1], 0), spec(LANES, 0)],
        out_specs=spec(V.shape[-1], 0),
        scratch_shapes=[pltpu.VMEM((MLSTM_HEADS, LANES, MLSTM_V), F32),
                        pltpu.VMEM((MLSTM_HEADS, 1, LANES), F32),
                        pltpu.VMEM((MLSTM_HEADS, 1, LANES), F32)],
        compiler_params=_cparams(("parallel", "arbitrary")),
        name="mlstm_bwd" if rev else "mlstm_fwd",
    )(QK, QK, V, G)


def _readout_kernel(x_ref, gt_ref, hf_ref, hb_ref, o_ref, og_ref, w_ref, y_ref, a_scr):
    for hd in range(MLSTM_HEADS):
        cols = slice(hd * LANES, (hd + 1) * LANES)
        hsum = hf_ref[:, cols] + hb_ref[:, cols]
        hn = _rms(hsum, og_ref[:, cols], 1.0 / MLSTM_V)
        gate = 1.0 / (1.0 + jnp.exp(-o_ref[:, cols]))
        a_scr[:, cols] = (hn * gate).astype(BF16)
    y_ref[...] = x_ref[...] + gt_ref[...] * _dot(a_scr[...], w_ref[...])


def _readout(X, gt, Hf, Hb, O, out_g, w_out, *, n_out_batches):
    nb, t, d = X.shape
    tm = TM_OUT
    n = Hf.shape[-1]
    tile = lambda w: pl.BlockSpec((None, tm, w), lambda b, i: (b, i, 0))
    full = lambda a: pl.BlockSpec(a.shape, lambda b, i: (0,) * a.ndim)
    return pl.pallas_call(
        _readout_kernel,
        out_shape=jax.ShapeDtypeStruct((n_out_batches, t, d), F32),
        grid=(n_out_batches, t // tm),
        in_specs=[tile(d), pl.BlockSpec((None, 1, d), lambda b, i: (b, 0, 0)), tile(n), tile(n), tile(n),
                  full(out_g), full(w_out)],
        out_specs=tile(d),
        scratch_shapes=[pltpu.VMEM((tm, n), BF16)],
        compiler_params=_cparams(("parallel", "parallel")),
        name="readout",
    )(X, gt, Hf, Hb, O, out_g, w_out)


def kernel(x, c, ctx, c_ctx, ada_w, ada_b, norm1_g, norm2_g, ffn_w_up, ffn_conv_w, ffn_conv_b, ffn_w_down,
           att_w_in, mla_qa_g, mla_w_qb, mla_kva_g, mla_w_kvb, mla_q_g, mla_k_g, gqa_q_g, gqa_k_g, att_w_out,
           ml_w_in, ml_conv_w, ml_conv_b, ml_gate_b, ml_out_g, ml_w_out):
    B, T, D = x.shape
    ctx_len = ctx.shape[1]
    depth = ada_w.shape[0]
    assert B * ctx_len == T, "context sequences must tile one latent-length row of the token array"
    NB = B + 1

    X = jnp.concatenate([x, ctx.reshape(1, T, D)], axis=0)

    rows = -(-NB // SUBLANES) * SUBLANES
    cvec = jnp.zeros((rows, D), F32).at[:B].set(c).at[B].set(c_ctx)
    mod = _adaln(cvec, ada_w, ada_b)
    mod = mod.reshape(depth, rows, 6, 1, D).transpose(0, 2, 1, 3, 4)

    rope_m = _rope_tables(T, MLA_ROPE, MLA_NOPE)
    rope_g = _rope_tables(T, GQA_DIM, 0)

    for layer in range(depth):
        last = layer == depth - 1
        j = layer // 2
        sh1, sc1, gt1, sh2, sc2, gt2 = (mod[layer, i] for i in range(6))
        g1 = norm1_g[layer][None, :]
        g2 = norm2_g[layer][None, :]
        if layer % 2 == 0:
            ew = _even_weights(att_w_in[j], mla_w_qb[j], mla_w_kvb[j], mla_q_g[j], mla_k_g[j],
                               gqa_q_g[j], gqa_k_g[j])
            qm, km, vm, qg, kg, vg = _even_in(X, sh1, sc1, g1, ew, mla_qa_g[j][None, :], mla_kva_g[j][None, :],
                                              rope_m, rope_g)
            oa = _attention(qm, km, vm, n_heads=2, shared_k=False, ctx_len=ctx_len)
            ob = _attention(qg, kg, vg, n_heads=GQA_HEADS // GQA_KV_HEADS, shared_k=True, ctx_len=ctx_len)
            X = _even_out(X, gt1, oa, ob, att_w_out[j].astype(BF16))
        else:
            gate_b = _pad_cols(ml_gate_b[j][None, :], LANES)
            w_in = jnp.pad(ml_w_in[j], ((0, 0), (0, LANES - 4 * MLSTM_HEADS))).astype(BF16)
            qk_pre, v, o, gates = _odd_in(X, sh1, sc1, g1, w_in, gate_b)
            qk = _qk_conv(qk_pre, ml_conv_w[j], ml_conv_b[j][None, :], ctx_len=ctx_len)
            hf = _mlstm(qk, v, gates, rev=False, ctx_len=ctx_len)
            hb = _mlstm(qk, v, gates, rev=True, ctx_len=ctx_len)
            X = _readout(X, gt1, hf, hb, o, ml_out_g[j].reshape(1, -1), ml_w_out[j].astype(BF16),
                         n_out_batches=B if last else NB)
        w_up = ffn_w_up[layer].astype(BF16)
        X = _ffn(X, sh2, sc2, gt2, g2, w_up[:, :D_FF], w_up[:, D_FF:], ffn_conv_w[layer],
                 ffn_conv_b[layer][None, :], ffn_w_down[layer].astype(BF16),
                 n_out_batches=B if last else NB, n_lat_batches=B, ctx_len=ctx_len)
    return X[:B]
```

```python
import functools

import numpy as np
import jax
import jax.numpy as jnp
from jax import lax
from jax.experimental import pallas as pl
from jax.experimental.pallas import tpu as pltpu

F32 = jnp.float32
BF16 = jnp.bfloat16

EPS = 1e-6
ROPE_THETA = 10000.0
GRID_W = 64
LOG2E = 1.4426950408889634

MLA_HEADS = 8
MLA_Q_RANK = 384
MLA_KV_RANK = 256
MLA_NOPE = 64
MLA_ROPE = 32
MLA_V = 64
MLA_QK = MLA_NOPE + MLA_ROPE
GQA_HEADS = 8
GQA_KV_HEADS = 2
GQA_DIM = 64
MLSTM_HEADS = 8
MLSTM_QK = 64
MLSTM_V = 128
D_FF = 2816

LANES = 128
SUBLANES = 8
VMEM_LIMIT = 56 * 1024 * 1024

TM_IN = 256
TM_OUT = 512
TM_FFN = 512
FF_CHUNK = 256
TQ = 256
KV_CHUNK = 256
ML_CHUNK = 128


def _cparams(sem):
    return pltpu.CompilerParams(dimension_semantics=sem, vmem_limit_bytes=VMEM_LIMIT)


def _dot(a, b):
    return jnp.dot(a, b, preferred_element_type=F32)


def _dot_nt(a, b):
    return lax.dot_general(a, b, (((1,), (1,)), ((), ())), preferred_element_type=F32)


def _dot_tn(a, b):
    return lax.dot_general(a, b, (((0,), (0,)), ((), ())), preferred_element_type=F32)


def _rms(x, gain_row, inv_n):
    ss = jnp.sum(x * x, axis=-1, keepdims=True) * inv_n
    return x * lax.rsqrt(ss + EPS) * gain_row


def _norm_mod(x, g_row, sh_row, sc_row):
    ms = jnp.mean(x * x, axis=-1, keepdims=True)
    return x * lax.rsqrt(ms + EPS) * (g_row * (1.0 + sc_row)) + sh_row


def _silu(x):
    return x * (1.0 / (1.0 + jnp.exp(-x)))


def _edge_masks(row0, n_rows, is_latent, seq_len, ctx_len):
    assert seq_len & (seq_len - 1) == 0 and ctx_len & (ctx_len - 1) == 0, "sequence lengths must be powers of two"
    last = jnp.where(is_latent, seq_len - 1, ctx_len - 1)
    pos = (row0 + lax.broadcasted_iota(jnp.int32, (n_rows, 1), 0)) & last
    return pos != 0, pos != last


def _adaln_kernel(c_ref, w_ref, b_ref, o_ref):
    c = _silu(c_ref[...]).astype(BF16)
    o_ref[...] = _dot(c, w_ref[...].astype(BF16)) + b_ref[...]


def _adaln(cvec, ada_w, ada_b):
    depth, d, n = ada_w.shape
    rows = cvec.shape[0]
    tn = 1536
    return pl.pallas_call(
        _adaln_kernel,
        out_shape=jax.ShapeDtypeStruct((depth, rows, n), F32),
        grid=(depth, n // tn),
        in_specs=[
            pl.BlockSpec((rows, d), lambda l, j: (0, 0)),
            pl.BlockSpec((None, d, tn), lambda l, j: (l, 0, j)),
            pl.BlockSpec((None, 1, tn), lambda l, j: (l, 0, j)),
        ],
        out_specs=pl.BlockSpec((None, rows, tn), lambda l, j: (l, 0, j)),
        compiler_params=_cparams(("arbitrary", "arbitrary")),
        name="adaln",
    )(cvec, ada_w, ada_b.reshape(depth, 1, n))


def _rope(y, cos, sa, sb, shift):
    return y * cos + pltpu.roll(y, LANES - shift, 1) * sa + pltpu.roll(y, shift, 1) * sb


def _pair_sumsq(y):
    n = y.shape[1]
    r = lax.broadcasted_iota(jnp.int32, (n, n), 0) // LANES
    c = lax.broadcasted_iota(jnp.int32, (n, n), 1) // LANES
    return _dot((y * y).astype(BF16), jnp.where(r == c, 1.0, 0.0).astype(BF16))


def _rope_pair(y, tabs, shift):
    cos, sa, sb = tabs
    return jnp.concatenate([_rope(y[:, :LANES], cos, sa, sb, shift), _rope(y[:, LANES:], cos, sa, sb, shift)], axis=1)


def _even_in_kernel(x_ref, sh_ref, sc_ref, g1_ref, win_ref, gqa_ref, wqb_ref, gkva_ref, wk_ref, wv_ref,
                    gmq_ref, gmk_ref, ggq_ref, ggk_ref, rm_ref, rg_ref,
                    qm_ref, km_ref, vm_ref, qg_ref, kg_ref, vg_ref):
    h = _norm_mod(x_ref[...], g1_ref[...], sh_ref[...], sc_ref[...]).astype(BF16)
    rope_m = (rm_ref[0], rm_ref[1], rm_ref[2])
    rope_g = (rg_ref[0], rg_ref[1], rg_ref[2])
    pw = 2 * LANES
    o_ckv = MLA_Q_RANK
    o_kr = o_ckv + MLA_KV_RANK
    o_gq = o_kr + LANES
    o_gk = o_gq + GQA_HEADS * LANES
    o_gv = o_gk + GQA_KV_HEADS * LANES
    two = lambda a: jnp.concatenate([a, a], axis=1)

    low = _dot(h, win_ref[:, 0:o_gq])
    cqn = _rms(low[:, 0:o_ckv], gqa_ref[...], 1.0 / MLA_Q_RANK).astype(BF16)
    for p in range(MLA_HEADS // 2):
        blk = _dot(cqn, wqb_ref[:, p * pw:(p + 1) * pw])
        y = blk * lax.rsqrt(_pair_sumsq(blk) * (1.0 / MLA_QK) + EPS) * gmq_ref[...]
        qm_ref[:, p * pw:(p + 1) * pw] = _rope_pair(y, rope_m, MLA_ROPE // 4).astype(BF16)

    ckvn = _rms(low[:, o_ckv:o_kr], gkva_ref[...], 1.0 / MLA_KV_RANK).astype(BF16)
    krp = low[:, o_kr:o_gq]
    gmk = gmk_ref[...]
    kr_rot = two(_rope(krp * gmk[:, :LANES], *rope_m, MLA_ROPE // 4))
    kr_ss = _pair_sumsq(two(krp))
    for p in range(MLA_HEADS // 2):
        blk = _dot(ckvn, wk_ref[:, p * pw:(p + 1) * pw])
        r = lax.rsqrt((_pair_sumsq(blk) + kr_ss) * (1.0 / MLA_QK) + EPS)
        km_ref[:, p * pw:(p + 1) * pw] = (r * (blk * gmk + kr_rot)).astype(BF16)
    vm_ref[...] = _dot(ckvn, wv_ref[...]).astype(BF16)

    for p in range(GQA_HEADS // 2):
        blk = _dot(h, win_ref[:, o_gq + p * pw:o_gq + (p + 1) * pw])
        y = blk * lax.rsqrt(_pair_sumsq(blk) * (1.0 / GQA_DIM) + EPS) * ggq_ref[...]
        qg_ref[:, p * pw:(p + 1) * pw] = _rope_pair(y, rope_g, GQA_DIM // 4).astype(BF16)
    for p in range(GQA_KV_HEADS // 2):
        blk = _dot(h, win_ref[:, o_gk + p * pw:o_gk + (p + 1) * pw])
        y = blk * lax.rsqrt(_pair_sumsq(blk) * (1.0 / GQA_DIM) + EPS) * ggk_ref[...]
        kg_ref[:, p * pw:(p + 1) * pw] = _rope_pair(y, rope_g, GQA_DIM // 4).astype(BF16)
    vg_ref[...] = _dot(h, win_ref[:, o_gv:o_gv + GQA_KV_HEADS * LANES]).astype(BF16)


def _pad_cols(w, width):
    return jnp.pad(w, ((0, 0), (0, width - w.shape[1])))


def _even_weights(w_in, w_qb, w_kvb, g_mq, g_mk, g_gq, g_gk):
    d = w_in.shape[0]
    s0 = MLA_Q_RANK
    s1 = s0 + MLA_KV_RANK
    s2 = s1 + MLA_ROPE
    s3 = s2 + GQA_HEADS * GQA_DIM
    s4 = s3 + GQA_KV_HEADS * GQA_DIM
    cq, ckv, kr = w_in[:, :s0], w_in[:, s0:s1], w_in[:, s1:s2]
    gq, gk, gv = w_in[:, s2:s3], w_in[:, s3:s4], w_in[:, s4:]
    kr_p = jnp.pad(kr, ((0, 0), (MLA_NOPE, LANES - MLA_QK)))
    gq_p = jnp.pad(gq.reshape(d, GQA_HEADS, GQA_DIM), ((0, 0), (0, 0), (0, LANES - GQA_DIM))).reshape(d, -1)
    gk_p = jnp.pad(gk.reshape(d, GQA_KV_HEADS, GQA_DIM), ((0, 0), (0, 0), (0, LANES - GQA_DIM))).reshape(d, -1)
    gv3 = gv.reshape(d, GQA_KV_HEADS, GQA_DIM)
    gv_p = jnp.concatenate([gv3, gv3], axis=-1).reshape(d, -1)
    win = jnp.concatenate([cq, ckv, kr_p, gq_p, gk_p, gv_p], axis=1).astype(BF16)
    wqb = jnp.pad(w_qb.reshape(MLA_Q_RANK, MLA_HEADS, MLA_QK),
                  ((0, 0), (0, 0), (0, LANES - MLA_QK))).reshape(MLA_Q_RANK, -1).astype(BF16)
    kv3 = w_kvb.reshape(MLA_KV_RANK, MLA_HEADS, MLA_NOPE + MLA_V)
    wk = jnp.pad(kv3[:, :, :MLA_NOPE], ((0, 0), (0, 0), (0, LANES - MLA_NOPE))).reshape(MLA_KV_RANK, -1).astype(BF16)
    wv = kv3[:, :, MLA_NOPE:].reshape(MLA_KV_RANK, -1).astype(BF16)
    pair = lambda g: jnp.tile(_pad_cols(g[None, :], LANES), (1, 2))
    gmq = pair(g_mq * (MLA_QK ** -0.5 * LOG2E))
    gmk = pair(g_mk)
    ggq = pair(g_gq * (GQA_DIM ** -0.5 * LOG2E))
    ggk = pair(g_gk)
    return win, wqb, wk, wv, gmq, gmk, ggq, ggk


def _rope_tables(n_tokens, rot_dim, offset):
    rows = n_tokens // GRID_W
    row = jnp.repeat(jnp.arange(rows), GRID_W).astype(F32)
    col = jnp.tile(jnp.arange(GRID_W), rows).astype(F32)
    n_freq = rot_dim // 4
    inv = ROPE_THETA ** (-jnp.arange(n_freq, dtype=F32) / n_freq)
    a_r, a_c = row[:, None] * inv, col[:, None] * inv
    ang = jnp.concatenate([a_r, a_r, a_c, a_c], axis=-1)
    cos, sin = jnp.cos(ang), jnp.sin(ang)
    quarter = (np.arange(rot_dim) // n_freq) % 2
    sa = jnp.where(quarter == 0, -sin, 0.0)
    sb = jnp.where(quarter == 1, sin, 0.0)
    pad = ((0, 0), (offset, LANES - offset - rot_dim))
    lat = jnp.stack([jnp.pad(cos, pad, constant_values=1.0), jnp.pad(sa, pad), jnp.pad(sb, pad)])
    ident = jnp.stack([jnp.ones((n_tokens, LANES), F32), jnp.zeros((n_tokens, LANES), F32),
                       jnp.zeros((n_tokens, LANES), F32)])
    return jnp.stack([lat, ident])


def _even_in(X, sh, sc, g1, ew, gqa, gkva, rope_m, rope_g):
    nb, t, d = X.shape
    win, wqb, wk, wv, gmq, gmk, ggq, ggk = ew
    tm = TM_IN
    full = lambda a: pl.BlockSpec(a.shape, lambda b, i: (0,) * a.ndim)
    row = pl.BlockSpec((None, 1, d), lambda b, i: (b, 0, 0))
    rope_spec = pl.BlockSpec((None, 3, tm, LANES), lambda b, i: (jnp.where(b == nb - 1, 1, 0), 0, i, 0))

    def out(width):
        return jax.ShapeDtypeStruct((nb, t, width), BF16), pl.BlockSpec((None, tm, width), lambda b, i: (b, i, 0))

    outs = [out(MLA_HEADS * LANES), out(MLA_HEADS * LANES), out(MLA_HEADS * MLA_V),
            out(GQA_HEADS * LANES), out(GQA_KV_HEADS * LANES), out(GQA_KV_HEADS * LANES)]
    return pl.pallas_call(
        _even_in_kernel,
        out_shape=[o[0] for o in outs],
        grid=(nb, t // tm),
        in_specs=[pl.BlockSpec((None, tm, d), lambda b, i: (b, i, 0)), row, row, full(g1), full(win), full(gqa),
                  full(wqb), full(gkva), full(wk), full(wv), full(gmq), full(gmk), full(ggq), full(ggk),
                  rope_spec, rope_spec],
        out_specs=[o[1] for o in outs],
        compiler_params=_cparams(("parallel", "parallel")),
        name="even_in",
    )(X, sh, sc, g1, win, gqa, wqb, gkva, wk, wv, gmq, gmk, ggq, ggk, rope_m, rope_g)


def _attn_kernel(q_ref, kl_ref, kc_ref, vl_ref, vc_ref, o_ref, s_scr, *, n_heads, shared_k, n_lat_batches):
    tq = q_ref.shape[0]
    lane = lax.broadcasted_iota(jnp.int32, (tq, LANES), 1)

    def run(parts):
        outs = []
        for hd in range(n_heads):
            q = q_ref[:, hd * LANES:(hd + 1) * LANES]
            kc0 = 0 if shared_k else hd * LANES
            mx = jnp.full((tq, LANES), -jnp.inf, F32)
            off = 0
            for k_ref, _, n_keys in parts:
                for c in range(n_keys // KV_CHUNK):
                    kb = k_ref[c * KV_CHUNK:(c + 1) * KV_CHUNK, kc0:kc0 + LANES]
                    s = _dot_nt(q, kb)
                    s_scr[:, off:off + KV_CHUNK] = s
                    for j in range(KV_CHUNK // LANES):
                        mx = jnp.maximum(mx, s[:, j * LANES:(j + 1) * LANES])
                    off += KV_CHUNK
            m = jnp.max(mx, axis=-1, keepdims=True)
            lsum = jnp.zeros((tq, LANES), F32)
            acc = jnp.zeros((tq, LANES), F32)
            off = 0
            for _, v_ref, n_keys in parts:
                for c in range(n_keys // KV_CHUNK):
                    p = jnp.exp2(s_scr[:, off:off + KV_CHUNK] - m)
                    for j in range(KV_CHUNK // LANES):
                        lsum = lsum + p[:, j * LANES:(j + 1) * LANES]
                    acc = acc + _dot(p.astype(BF16), v_ref[c * KV_CHUNK:(c + 1) * KV_CHUNK, :])
                    off += KV_CHUNK
            outs.append(acc * (1.0 / jnp.sum(lsum, axis=-1, keepdims=True)))
        for j in range(n_heads // 2):
            pair = jnp.where(lane < LANES // 2, outs[2 * j], outs[2 * j + 1])
            o_ref[:, j * LANES:(j + 1) * LANES] = pair.astype(o_ref.dtype)

    n_ctx = kc_ref.shape[0]
    n_lat = kl_ref.shape[0]
    is_lat = pl.program_id(0) < n_lat_batches

    @pl.when(is_lat)
    def _():
        run([(kc_ref, vc_ref, n_ctx), (kl_ref, vl_ref, n_lat)])

    @pl.when(jnp.logical_not(is_lat))
    def _():
        run([(kc_ref, vc_ref, n_ctx)])


def _attention(Q, K, V, *, n_heads, shared_k, ctx_len):
    nb, t, _ = Q.shape
    nbl = nb - 1
    assert ctx_len == TQ and nbl * ctx_len == t
    nk = 1 if shared_k else n_heads
    groups = V.shape[-1] // LANES
    qw, kw, ow = n_heads * LANES, nk * LANES, n_heads * LANES // 2
    lat_b = lambda b: jnp.minimum(b, nbl - 1)
    ctx_blk = lambda b, i: jnp.where(b < nbl, b, i)
    return pl.pallas_call(
        functools.partial(_attn_kernel, n_heads=n_heads, shared_k=shared_k, n_lat_batches=nbl),
        out_shape=jax.ShapeDtypeStruct((nb, t, groups * ow), BF16),
        grid=(nb, groups, t // TQ),
        in_specs=[
            pl.BlockSpec((None, TQ, qw), lambda b, g, i: (b, i, g)),
            pl.BlockSpec((None, t, kw), lambda b, g, i: (lat_b(b), 0, g)),
            pl.BlockSpec((None, ctx_len, kw), lambda b, g, i: (nbl, ctx_blk(b, i), g)),
            pl.BlockSpec((None, t, LANES), lambda b, g, i: (lat_b(b), 0, g)),
            pl.BlockSpec((None, ctx_len, LANES), lambda b, g, i: (nbl, ctx_blk(b, i), g)),
        ],
        out_specs=pl.BlockSpec((None, TQ, ow), lambda b, g, i: (b, i, g)),
        scratch_shapes=[pltpu.VMEM((TQ, t + ctx_len), F32)],
        compiler_params=_cparams(("parallel", "parallel", "arbitrary")),
        name="attn_shared_k" if shared_k else "attn",
    )(Q, K, K, V, V)


def _even_out_kernel(x_ref, gt_ref, oa_ref, ob_ref, w_ref, y_ref):
    na = oa_ref.shape[1]
    out = _dot(oa_ref[...], w_ref[0:na, :]) + _dot(ob_ref[...], w_ref[na:, :])
    y_ref[...] = x_ref[...] + gt_ref[...] * out


def _even_out(X, gt, Oa, Ob, w_out):
    nb, t, d = X.shape
    tm = TM_OUT
    tile = lambda w: pl.BlockSpec((None, tm, w), lambda b, i: (b, i, 0))
    return pl.pallas_call(
        _even_out_kernel,
        out_shape=jax.ShapeDtypeStruct((nb, t, d), F32),
        grid=(nb, t // tm),
        in_specs=[tile(d), pl.BlockSpec((None, 1, d), lambda b, i: (b, 0, 0)), tile(Oa.shape[-1]), tile(Ob.shape[-1]),
                  pl.BlockSpec(w_out.shape, lambda b, i: (0, 0))],
        out_specs=tile(d),
        compiler_params=_cparams(("parallel", "parallel")),
        name="even_out",
    )(X, gt, Oa, Ob, w_out)


def _ffn_kernel(x_ref, xp_ref, xn_ref, sh_ref, sc_ref, gt_ref, g2_ref, wg_ref, wv_ref, cw_ref, cb_ref, wd_ref,
                y_ref, a_scr, *, seq_len, ctx_len, n_lat_batches):
    tm = x_ref.shape[0]
    text = tm + 2 * SUBLANES
    x = x_ref[...]
    g2, sh, sc = g2_ref[...], sh_ref[...], sc_ref[...]
    xext = jnp.concatenate([xp_ref[...], x, xn_ref[...]], axis=0)
    hext = _norm_mod(xext, g2, sh, sc).astype(BF16)
    has_prev, has_next = _edge_masks(pl.program_id(1) * tm, tm, pl.program_id(0) < n_lat_batches, seq_len, ctx_len)
    mid = slice(SUBLANES, SUBLANES + tm)
    for j in range(D_FF // FF_CHUNK):
        cols = slice(j * FF_CHUNK, (j + 1) * FF_CHUNK)
        gext = _dot(hext, wg_ref[:, cols])
        val = _dot(hext, wv_ref[:, cols])[mid]
        prev = jnp.where(has_prev, pltpu.roll(gext, 1, 0)[mid], 0.0)
        nxt = jnp.where(has_next, pltpu.roll(gext, text - 1, 0)[mid], 0.0)
        conv = prev * cw_ref[0:1, cols] + gext[mid] * cw_ref[1:2, cols] + nxt * cw_ref[2:3, cols] + cb_ref[:, cols]
        a_scr[:, cols] = (_silu(conv) * val).astype(BF16)
    y_ref[...] = x + gt_ref[...] * _dot(a_scr[...], wd_ref[...])


def _ffn(X, sh, sc, gt, g2, wg, wv, cw, cb, wd, *, n_out_batches, n_lat_batches, ctx_len):
    nb, t, d = X.shape
    tm = TM_FFN
    nblk = t // SUBLANES
    per = tm // SUBLANES
    row = pl.BlockSpec((None, 1, d), lambda b, i: (b, 0, 0))
    full = lambda a: pl.BlockSpec(a.shape, lambda b, i: (0,) * a.ndim, pipeline_mode=pl.Buffered(1))
    return pl.pallas_call(
        functools.partial(_ffn_kernel, seq_len=t, ctx_len=ctx_len, n_lat_batches=n_lat_batches),
        out_shape=jax.ShapeDtypeStruct((n_out_batches, t, d), F32),
        grid=(n_out_batches, t // tm),
        in_specs=[
            pl.BlockSpec((None, tm, d), lambda b, i: (b, i, 0)),
            pl.BlockSpec((None, SUBLANES, d), lambda b, i: (b, jnp.maximum(i * per - 1, 0), 0)),
            pl.BlockSpec((None, SUBLANES, d), lambda b, i: (b, jnp.minimum((i + 1) * per, nblk - 1), 0)),
            row, row, row, full(g2), full(wg), full(wv), full(cw), full(cb), full(wd),
        ],
        out_specs=pl.BlockSpec((None, tm, d), lambda b, i: (b, i, 0)),
        scratch_shapes=[pltpu.VMEM((tm, D_FF), BF16)],
        compiler_params=_cparams(("parallel", "parallel")),
        name="conv_ffn",
    )(X, X, X, sh, sc, gt, g2, wg, wv, cw, cb, wd)


def _odd_in_kernel(x_ref, sh_ref, sc_ref, g1_ref, w_ref, gb_ref, qk_ref, v_ref, o_ref, g_ref):
    h = _norm_mod(x_ref[...], g1_ref[...], sh_ref[...], sc_ref[...]).astype(BF16)
    nqk = qk_ref.shape[1]
    nv = v_ref.shape[1]
    qk_ref[...] = _dot(h, w_ref[:, 0:nqk])
    v_ref[...] = _dot(h, w_ref[:, nqk:nqk + nv]).astype(BF16)
    o_ref[...] = _dot(h, w_ref[:, nqk + nv:nqk + 2 * nv])
    g_ref[...] = _dot(h, w_ref[:, nqk + 2 * nv:]) + gb_ref[...]


def _odd_in(X, sh, sc, g1, w, gate_b):
    nb, t, d = X.shape
    tm = TM_OUT
    nqk = 2 * MLSTM_HEADS * MLSTM_QK
    nv = MLSTM_HEADS * MLSTM_V
    row = pl.BlockSpec((None, 1, d), lambda b, i: (b, 0, 0))
    full = lambda a: pl.BlockSpec(a.shape, lambda b, i: (0,) * a.ndim)
    tile = lambda wd: pl.BlockSpec((None, tm, wd), lambda b, i: (b, i, 0))
    return pl.pallas_call(
        _odd_in_kernel,
        out_shape=[jax.ShapeDtypeStruct((nb, t, nqk), F32), jax.ShapeDtypeStruct((nb, t, nv), BF16),
                   jax.ShapeDtypeStruct((nb, t, nv), F32), jax.ShapeDtypeStruct((nb, t, LANES), F32)],
        grid=(nb, t // tm),
        in_specs=[tile(d), row, row, full(g1), full(w), full(gate_b)],
        out_specs=[tile(nqk), tile(nv), tile(nv), tile(LANES)],
        compiler_params=_cparams(("parallel", "parallel")),
        name="odd_in",
    )(X, sh, sc, g1, w, gate_b)


def _qk_conv_kernel(x_ref, cw_ref, cb_ref, y_ref, *, ctx_len, n_lat_batches, n_q_blocks):
    t = x_ref.shape[0]
    x = x_ref[...]
    has_prev, has_next = _edge_masks(0, t, pl.program_id(0) < n_lat_batches, t, ctx_len)
    prev = jnp.where(has_prev, pltpu.roll(x, 1, 0), 0.0)
    nxt = jnp.where(has_next, pltpu.roll(x, t - 1, 0), 0.0)
    conv = prev * cw_ref[0:1, :] + x * cw_ref[1:2, :] + nxt * cw_ref[2:3, :] + cb_ref[...]
    scale = jnp.where(pl.program_id(1) >= n_q_blocks, MLSTM_QK ** -0.5, 1.0)
    y_ref[...] = (_silu(conv) * scale).astype(BF16)


def _qk_conv(QK, cw, cb, *, ctx_len):
    nb, t, n = QK.shape
    return pl.pallas_call(
        functools.partial(_qk_conv_kernel, ctx_len=ctx_len, n_lat_batches=nb - 1, n_q_blocks=n // (2 * LANES)),
        out_shape=jax.ShapeDtypeStruct((nb, t, n), BF16),
        grid=(nb, n // LANES),
        in_specs=[pl.BlockSpec((None, t, LANES), lambda b, j: (b, 0, j)),
                  pl.BlockSpec((cw.shape[0], LANES), lambda b, j: (0, j)),
                  pl.BlockSpec((1, LANES), lambda b, j: (0, j))],
        out_specs=pl.BlockSpec((None, t, LANES), lambda b, j: (b, 0, j)),
        compiler_params=_cparams(("parallel", "parallel")),
        name="qk_conv",
    )(QK, cw, cb)


def _log_sigmoid(x):
    return jnp.minimum(x, 0.0) - jnp.log(1.0 + jnp.exp(-jnp.abs(x)))


def _running_max(x, rev):
    n = x.shape[0]
    row = lax.broadcasted_iota(jnp.int32, x.shape, 0)
    k = 1
    while k < n:
        if rev:
            x = jnp.where(row < n - k, jnp.maximum(x, pltpu.roll(x, n - k, 0)), x)
        else:
            x = jnp.where(row >= k, jnp.maximum(x, pltpu.roll(x, k, 0)), x)
        k *= 2
    return x


def _mlstm_kernel(q_ref, k_ref, v_ref, g_ref, h_ref, cn_scr, m_scr, *, rev):
    @pl.when(pl.program_id(1) == 0)
    def _():
        cn_scr[...] = jnp.zeros_like(cn_scr)
        m_scr[...] = jnp.zeros_like(m_scr)

    L = q_ref.shape[0]
    nh = MLSTM_HEADS
    i_off = 2 * nh if rev else 0
    last = 0 if rev else L - 1
    g = g_ref[...]
    r_i = lax.broadcasted_iota(jnp.int32, (L, L), 0)
    c_i = lax.broadcasted_iota(jnp.int32, (L, L), 1)
    tri = (c_i >= r_i) if rev else (c_i <= r_i)
    b_f = jnp.dot(tri.astype(F32), _log_sigmoid(g), precision=lax.Precision.HIGHEST, preferred_element_type=F32)
    b = pltpu.roll(b_f, LANES - nh, 1)
    u = g - b
    m_prev = m_scr[...]
    mm = jnp.maximum(m_prev, _running_max(u, rev))
    em = jnp.exp(-(b + mm))
    b_last = b[last:last + 1, :]
    a = b_last - b + g
    m_new = jnp.maximum(b_last + m_prev, jnp.max(a, axis=0, keepdims=True))
    decay = jnp.exp(b_last + m_prev - m_new)
    w = jnp.exp(a - m_new)
    m_scr[...] = m_new
    u_t = u.T
    w_t = w.T
    ones = jnp.ones((L, LANES), BF16)
    row = lax.broadcasted_iota(jnp.int32, (LANES, L), 0)
    for pair in range(nh // 2):
        qp = q_ref[:, pair * LANES:(pair + 1) * LANES]
        kt = k_ref[:, pair * LANES:(pair + 1) * LANES].astype(F32).T
        for half in range(2):
            hd = 2 * pair + half
            col = i_off + hd
            own = (row < MLSTM_QK) if half == 0 else (row >= MLSTM_QK)
            kt_h = jnp.where(own, kt, 0.0)
            v1 = jnp.concatenate([v_ref[:, hd * LANES:(hd + 1) * LANES], ones], axis=1)
            mm_b = jnp.broadcast_to(mm[:, col:col + 1], (L, LANES))
            em_b = jnp.broadcast_to(em[:, col:col + 1], (L, LANES))
            d = jnp.where(tri, jnp.exp(u_t[col:col + 1, :] - mm_b), 0.0)
            s = _dot(qp, kt_h.astype(BF16)) * d
            cn = cn_scr[hd]
            qc = _dot(qp, cn.astype(BF16))
            sv = _dot(s.astype(BF16), v1)
            w_inter = jnp.exp(m_prev[:, col:col + 1] - mm_b)
            num = w_inter * qc[:, :LANES] + sv[:, :LANES]
            den = w_inter * qc[:, LANES:] + sv[:, LANES:]
            h_ref[:, hd * LANES:(hd + 1) * LANES] = num * (1.0 / jnp.maximum(jnp.abs(den), em_b))
            ktw = (kt_h * w_t[col:col + 1, :]).astype(BF16)
            cn_scr[hd] = decay[:, col:col + 1] * cn + _dot(ktw, v1)


def _mlstm(QK, V, G, *, rev, ctx_len):
    nb, t, nqk = QK.shape
    nbl = nb - 1
    L = ML_CHUNK
    ncc, nlc = ctx_len // L, t // L
    nq = nqk // 2

    def blk(b, c):
        if rev:
            cc, lc = ncc - 1 - c, nlc - 1 - (c - ncc)
        else:
            cc, lc = c, c - ncc
        is_ctx = c < ncc
        return jnp.where(is_ctx, nbl, b), jnp.where(is_ctx, b * ncc + cc, lc)

    def spec(width, col):
        return pl.BlockSpec((None, L, width), lambda b, c: (*blk(b, c), col))

    return pl.pallas_call(
        functools.partial(_mlstm_kernel, rev=rev),
        out_shape=jax.ShapeDtypeStruct((nb, t, V.shape[-1]), F32),
        grid=(nbl, ncc + nlc),
        in_specs=[spec(nq, 0), spec(nq, 1), spec(V.shape[-1], 0), spec(LANES, 0)],
        out_specs=spec(V.shape[-1], 0),
        scratch_shapes=[pltpu.VMEM((MLSTM_HEADS, LANES, MLSTM_V + LANES), F32),
                        pltpu.VMEM((1, LANES), F32)],
        compiler_params=_cparams(("parallel", "arbitrary")),
        name="mlstm_bwd" if rev else "mlstm_fwd",
    )(QK, QK, V, G)


def _readout_kernel(x_ref, gt_ref, hf_ref, hb_ref, o_ref, og_ref, w_ref, y_ref, a_scr):
    for hd in range(MLSTM_HEADS):
        cols = slice(hd * LANES, (hd + 1) * LANES)
        hsum = hf_ref[:, cols] + hb_ref[:, cols]
        hn = _rms(hsum, og_ref[:, cols], 1.0 / MLSTM_V)
        gate = 1.0 / (1.0 + jnp.exp(-o_ref[:, cols]))
        a_scr[:, cols] = (hn * gate).astype(BF16)
    y_ref[...] = x_ref[...] + gt_ref[...] * _dot(a_scr[...], w_ref[...])


def _readout(X, gt, Hf, Hb, O, out_g, w_out, *, n_out_batches):
    nb, t, d = X.shape
    tm = TM_OUT
    n = Hf.shape[-1]
    tile = lambda w: pl.BlockSpec((None, tm, w), lambda b, i: (b, i, 0))
    full = lambda a: pl.BlockSpec(a.shape, lambda b, i: (0,) * a.ndim)
    return pl.pallas_call(
        _readout_kernel,
        out_shape=jax.ShapeDtypeStruct((n_out_batches, t, d), F32),
        grid=(n_out_batches, t // tm),
        in_specs=[tile(d), pl.BlockSpec((None, 1, d), lambda b, i: (b, 0, 0)), tile(n), tile(n), tile(n),
                  full(out_g), full(w_out)],
        out_specs=tile(d),
        scratch_shapes=[pltpu.VMEM((tm, n), BF16)],
        compiler_params=_cparams(("parallel", "parallel")),
        name="readout",
    )(X, gt, Hf, Hb, O, out_g, w_out)


def kernel(x, c, ctx, c_ctx, ada_w, ada_b, norm1_g, norm2_g, ffn_w_up, ffn_conv_w, ffn_conv_b, ffn_w_down,
           att_w_in, mla_qa_g, mla_w_qb, mla_kva_g, mla_w_kvb, mla_q_g, mla_k_g, gqa_q_g, gqa_k_g, att_w_out,
           ml_w_in, ml_conv_w, ml_conv_b, ml_gate_b, ml_out_g, ml_w_out):
    B, T, D = x.shape
    ctx_len = ctx.shape[1]
    depth = ada_w.shape[0]
    assert B * ctx_len == T, "context sequences must tile one latent-length row of the token array"
    NB = B + 1

    X = jnp.concatenate([x, ctx.reshape(1, T, D)], axis=0)

    rows = -(-NB // SUBLANES) * SUBLANES
    cvec = jnp.zeros((rows, D), F32).at[:B].set(c).at[B].set(c_ctx)
    mod = _adaln(cvec, ada_w, ada_b)
    mod = mod.reshape(depth, rows, 6, 1, D).transpose(0, 2, 1, 3, 4)

    rope_m = _rope_tables(T, MLA_ROPE, MLA_NOPE)
    rope_g = _rope_tables(T, GQA_DIM, 0)

    for layer in range(depth):
        last = layer == depth - 1
        j = layer // 2
        sh1, sc1, gt1, sh2, sc2, gt2 = (mod[layer, i] for i in range(6))
        g1 = norm1_g[layer][None, :]
        g2 = norm2_g[layer][None, :]
        if layer % 2 == 0:
            ew = _even_weights(att_w_in[j], mla_w_qb[j], mla_w_kvb[j], mla_q_g[j], mla_k_g[j],
                               gqa_q_g[j], gqa_k_g[j])
            qm, km, vm, qg, kg, vg = _even_in(X, sh1, sc1, g1, ew, mla_qa_g[j][None, :], mla_kva_g[j][None, :],
                                              rope_m, rope_g)
            oa = _attention(qm, km, vm, n_heads=2, shared_k=False, ctx_len=ctx_len)
            ob = _attention(qg, kg, vg, n_heads=GQA_HEADS // GQA_KV_HEADS, shared_k=True, ctx_len=ctx_len)
            X = _even_out(X, gt1, oa, ob, att_w_out[j].astype(BF16))
        else:
            gate_b = _pad_cols(ml_gate_b[j][None, :], LANES)
            w_in = jnp.pad(ml_w_in[j], ((0, 0), (0, LANES - 4 * MLSTM_HEADS))).astype(BF16)
            qk_pre, v, o, gates = _odd_in(X, sh1, sc1, g1, w_in, gate_b)
            qk = _qk_conv(qk_pre, ml_conv_w[j], ml_conv_b[j][None, :], ctx_len=ctx_len)
            hf = _mlstm(qk, v, gates, rev=False, ctx_len=ctx_len)
            hb = _mlstm(qk, v, gates, rev=True, ctx_len=ctx_len)
            X = _readout(X, gt1, hf, hb, o, ml_out_g[j].reshape(1, -1), ml_w_out[j].astype(BF16),
                         n_out_batches=B if last else NB)
        w_up = ffn_w_up[layer].astype(BF16)
        X = _ffn(X, sh2, sc2, gt2, g2, w_up[:, :D_FF], w_up[:, D_FF:], ffn_conv_w[layer],
                 ffn_conv_b[layer][None, :], ffn_w_down[layer].astype(BF16),
                 n_out_batches=B if last else NB, n_lat_batches=B, ctx_len=ctx_len)
    return X[:B]
```

```python
import functools

import numpy as np
import jax
import jax.numpy as jnp
from jax import lax
from jax.experimental import pallas as pl
from jax.experimental.pallas import tpu as pltpu

F32 = jnp.float32
BF16 = jnp.bfloat16

EPS = 1e-6
ROPE_THETA = 10000.0
GRID_W = 64
LOG2E = 1.4426950408889634

MLA_HEADS = 8
MLA_Q_RANK = 384
MLA_KV_RANK = 256
MLA_NOPE = 64
MLA_ROPE = 32
MLA_V = 64
MLA_QK = MLA_NOPE + MLA_ROPE
GQA_HEADS = 8
GQA_KV_HEADS = 2
GQA_DIM = 64
MLSTM_HEADS = 8
MLSTM_QK = 64
MLSTM_V = 128
D_FF = 2816

LANES = 128
SUBLANES = 8
VMEM_LIMIT = 56 * 1024 * 1024

TM_IN = 256
TM_OUT = 512
TM_FFN = 512
FF_CHUNK = 256
TQ = 256
KV_CHUNK = 512
ML_CHUNK = 128


def _cparams(sem):
    return pltpu.CompilerParams(dimension_semantics=sem, vmem_limit_bytes=VMEM_LIMIT)


def _dot(a, b):
    return jnp.dot(a, b, preferred_element_type=F32)


def _dot_nt(a, b):
    return lax.dot_general(a, b, (((1,), (1,)), ((), ())), preferred_element_type=F32)


def _dot_tn(a, b):
    return lax.dot_general(a, b, (((0,), (0,)), ((), ())), preferred_element_type=F32)


def _rms(x, gain_row, inv_n):
    ss = jnp.sum(x * x, axis=-1, keepdims=True) * inv_n
    return x * lax.rsqrt(ss + EPS) * gain_row


def _norm_mod(x, g_row, sh_row, sc_row):
    ms = jnp.mean(x * x, axis=-1, keepdims=True)
    return x * lax.rsqrt(ms + EPS) * (g_row * (1.0 + sc_row)) + sh_row


def _silu(x):
    return x * (1.0 / (1.0 + jnp.exp(-x)))


def _edge_masks(row0, n_rows, is_latent, seq_len, ctx_len):
    assert seq_len & (seq_len - 1) == 0 and ctx_len & (ctx_len - 1) == 0, "sequence lengths must be powers of two"
    last = jnp.where(is_latent, seq_len - 1, ctx_len - 1)
    pos = (row0 + lax.broadcasted_iota(jnp.int32, (n_rows, 1), 0)) & last
    return pos != 0, pos != last


def _adaln_kernel(c_ref, w_ref, b_ref, o_ref):
    c = _silu(c_ref[...]).astype(BF16)
    o_ref[...] = _dot(c, w_ref[...].astype(BF16)) + b_ref[...]


def _adaln(cvec, ada_w, ada_b):
    depth, d, n = ada_w.shape
    rows = cvec.shape[0]
    tn = 1536
    return pl.pallas_call(
        _adaln_kernel,
        out_shape=jax.ShapeDtypeStruct((depth, rows, n), F32),
        grid=(depth, n // tn),
        in_specs=[
            pl.BlockSpec((rows, d), lambda l, j: (0, 0)),
            pl.BlockSpec((None, d, tn), lambda l, j: (l, 0, j)),
            pl.BlockSpec((None, 1, tn), lambda l, j: (l, 0, j)),
        ],
        out_specs=pl.BlockSpec((None, rows, tn), lambda l, j: (l, 0, j)),
        compiler_params=_cparams(("arbitrary", "arbitrary")),
        name="adaln",
    )(cvec, ada_w, ada_b.reshape(depth, 1, n))


def _rope(y, cos, sa, sb, shift):
    return y * cos + pltpu.roll(y, LANES - shift, 1) * sa + pltpu.roll(y, shift, 1) * sb


def _pair_sumsq(y):
    n = y.shape[1]
    r = lax.broadcasted_iota(jnp.int32, (n, n), 0) // LANES
    c = lax.broadcasted_iota(jnp.int32, (n, n), 1) // LANES
    return _dot((y * y).astype(BF16), jnp.where(r == c, 1.0, 0.0).astype(BF16))


def _rope_pair(y, tabs, shift):
    cos, sa, sb = tabs
    return jnp.concatenate([_rope(y[:, :LANES], cos, sa, sb, shift), _rope(y[:, LANES:], cos, sa, sb, shift)], axis=1)


def _even_in_kernel(x_ref, sh_ref, sc_ref, g1_ref, win_ref, gqa_ref, wqb_ref, gkva_ref, wk_ref, wv_ref,
                    gmq_ref, gmk_ref, ggq_ref, ggk_ref, rm_ref, rg_ref,
                    qm_ref, km_ref, vm_ref, qg_ref, kg_ref, vg_ref):
    h = _norm_mod(x_ref[...], g1_ref[...], sh_ref[...], sc_ref[...]).astype(BF16)
    rope_m = (rm_ref[0], rm_ref[1], rm_ref[2])
    rope_g = (rg_ref[0], rg_ref[1], rg_ref[2])
    pw = 2 * LANES
    o_ckv = MLA_Q_RANK
    o_kr = o_ckv + MLA_KV_RANK
    o_gq = o_kr + LANES
    o_gk = o_gq + GQA_HEADS * LANES
    o_gv = o_gk + GQA_KV_HEADS * LANES
    two = lambda a: jnp.concatenate([a, a], axis=1)

    low = _dot(h, win_ref[:, 0:o_gq])
    cqn = _rms(low[:, 0:o_ckv], gqa_ref[...], 1.0 / MLA_Q_RANK).astype(BF16)
    ckvn = _rms(low[:, o_ckv:o_kr], gkva_ref[...], 1.0 / MLA_KV_RANK).astype(BF16)
    krp = low[:, o_kr:o_gq]
    gmk = gmk_ref[...]
    kr_rot = two(_rope(krp * gmk[:, :LANES], *rope_m, MLA_ROPE // 4))
    kr_ss = _pair_sumsq(two(krp))
    vm_ref[...] = _dot(ckvn, wv_ref[...]).astype(BF16)
    vg_ref[...] = _dot(h, win_ref[:, o_gv:o_gv + GQA_KV_HEADS * LANES]).astype(BF16)

    def normed_roped(lhs, w_ref, c0, gain_ref, inv_n, tabs, shift, out_ref, p):
        blk = _dot(lhs, w_ref[:, c0 + p * pw:c0 + (p + 1) * pw])
        yield
        ss = _pair_sumsq(blk)
        yield
        y = blk * lax.rsqrt(ss * inv_n + EPS) * gain_ref[...]
        yield
        out_ref[:, p * pw:(p + 1) * pw] = _rope_pair(y, tabs, shift).astype(BF16)

    def mla_key(p):
        blk = _dot(ckvn, wk_ref[:, p * pw:(p + 1) * pw])
        yield
        ss = _pair_sumsq(blk)
        yield
        r = lax.rsqrt((ss + kr_ss) * (1.0 / MLA_QK) + EPS)
        yield
        km_ref[:, p * pw:(p + 1) * pw] = (r * (blk * gmk + kr_rot)).astype(BF16)

    for p in range(MLA_HEADS // 2):
        chains = [normed_roped(cqn, wqb_ref, 0, gmq_ref, 1.0 / MLA_QK, rope_m, MLA_ROPE // 4, qm_ref, p),
                  mla_key(p),
                  normed_roped(h, win_ref, o_gq, ggq_ref, 1.0 / GQA_DIM, rope_g, GQA_DIM // 4, qg_ref, p)]
        if p < GQA_KV_HEADS // 2:
            chains.append(normed_roped(h, win_ref, o_gk, ggk_ref, 1.0 / GQA_DIM, rope_g, GQA_DIM // 4, kg_ref, p))
        _lockstep(chains)


def _lockstep(chains):
    chains = list(chains)
    while chains:
        for c in list(chains):
            try:
                next(c)
            except StopIteration:
                chains.remove(c)


def _pad_cols(w, width):
    return jnp.pad(w, ((0, 0), (0, width - w.shape[1])))


def _even_weights(w_in, w_qb, w_kvb, g_mq, g_mk, g_gq, g_gk):
    d = w_in.shape[0]
    s0 = MLA_Q_RANK
    s1 = s0 + MLA_KV_RANK
    s2 = s1 + MLA_ROPE
    s3 = s2 + GQA_HEADS * GQA_DIM
    s4 = s3 + GQA_KV_HEADS * GQA_DIM
    cq, ckv, kr = w_in[:, :s0], w_in[:, s0:s1], w_in[:, s1:s2]
    gq, gk, gv = w_in[:, s2:s3], w_in[:, s3:s4], w_in[:, s4:]
    kr_p = jnp.pad(kr, ((0, 0), (MLA_NOPE, LANES - MLA_QK)))
    gq_p = jnp.pad(gq.reshape(d, GQA_HEADS, GQA_DIM), ((0, 0), (0, 0), (0, LANES - GQA_DIM))).reshape(d, -1)
    gk_p = jnp.pad(gk.reshape(d, GQA_KV_HEADS, GQA_DIM), ((0, 0), (0, 0), (0, LANES - GQA_DIM))).reshape(d, -1)
    gv3 = gv.reshape(d, GQA_KV_HEADS, GQA_DIM)
    gv_p = jnp.concatenate([gv3, gv3], axis=-1).reshape(d, -1)
    win = jnp.concatenate([cq, ckv, kr_p, gq_p, gk_p, gv_p], axis=1).astype(BF16)
    wqb = jnp.pad(w_qb.reshape(MLA_Q_RANK, MLA_HEADS, MLA_QK),
                  ((0, 0), (0, 0), (0, LANES - MLA_QK))).reshape(MLA_Q_RANK, -1).astype(BF16)
    kv3 = w_kvb.reshape(MLA_KV_RANK, MLA_HEADS, MLA_NOPE + MLA_V)
    wk = jnp.pad(kv3[:, :, :MLA_NOPE], ((0, 0), (0, 0), (0, LANES - MLA_NOPE))).reshape(MLA_KV_RANK, -1).astype(BF16)
    wv = kv3[:, :, MLA_NOPE:].reshape(MLA_KV_RANK, -1).astype(BF16)
    pair = lambda g: jnp.tile(_pad_cols(g[None, :], LANES), (1, 2))
    gmq = pair(g_mq * (MLA_QK ** -0.5 * LOG2E))
    gmk = pair(g_mk)
    ggq = pair(g_gq * (GQA_DIM ** -0.5 * LOG2E))
    ggk = pair(g_gk)
    return win, wqb, wk, wv, gmq, gmk, ggq, ggk


def _rope_tables(n_tokens, rot_dim, offset):
    rows = n_tokens // GRID_W
    row = jnp.repeat(jnp.arange(rows), GRID_W).astype(F32)
    col = jnp.tile(jnp.arange(GRID_W), rows).astype(F32)
    n_freq = rot_dim // 4
    inv = ROPE_THETA ** (-jnp.arange(n_freq, dtype=F32) / n_freq)
    a_r, a_c = row[:, None] * inv, col[:, None] * inv
    ang = jnp.concatenate([a_r, a_r, a_c, a_c], axis=-1)
    cos, sin = jnp.cos(ang), jnp.sin(ang)
    quarter = (np.arange(rot_dim) // n_freq) % 2
    sa = jnp.where(quarter == 0, -sin, 0.0)
    sb = jnp.where(quarter == 1, sin, 0.0)
    pad = ((0, 0), (offset, LANES - offset - rot_dim))
    lat = jnp.stack([jnp.pad(cos, pad, constant_values=1.0), jnp.pad(sa, pad), jnp.pad(sb, pad)])
    ident = jnp.stack([jnp.ones((n_tokens, LANES), F32), jnp.zeros((n_tokens, LANES), F32),
                       jnp.zeros((n_tokens, LANES), F32)])
    return jnp.stack([lat, ident])


def _even_in(X, sh, sc, g1, ew, gqa, gkva, rope_m, rope_g):
    nb, t, d = X.shape
    win, wqb, wk, wv, gmq, gmk, ggq, ggk = ew
    tm = TM_IN
    full = lambda a: pl.BlockSpec(a.shape, lambda b, i: (0,) * a.ndim)
    row = pl.BlockSpec((None, 1, d), lambda b, i: (b, 0, 0))
    rope_spec = pl.BlockSpec((None, 3, tm, LANES), lambda b, i: (jnp.where(b == nb - 1, 1, 0), 0, i, 0))

    def out(width):
        return jax.ShapeDtypeStruct((nb, t, width), BF16), pl.BlockSpec((None, tm, width), lambda b, i: (b, i, 0))

    outs = [out(MLA_HEADS * LANES), out(MLA_HEADS * LANES), out(MLA_HEADS * MLA_V),
            out(GQA_HEADS * LANES), out(GQA_KV_HEADS * LANES), out(GQA_KV_HEADS * LANES)]
    return pl.pallas_call(
        _even_in_kernel,
        out_shape=[o[0] for o in outs],
        grid=(nb, t // tm),
        in_specs=[pl.BlockSpec((None, tm, d), lambda b, i: (b, i, 0)), row, row, full(g1), full(win), full(gqa),
                  full(wqb), full(gkva), full(wk), full(wv), full(gmq), full(gmk), full(ggq), full(ggk),
                  rope_spec, rope_spec],
        out_specs=[o[1] for o in outs],
        compiler_params=_cparams(("parallel", "parallel")),
        name="even_in",
    )(X, sh, sc, g1, win, gqa, wqb, gkva, wk, wv, gmq, gmk, ggq, ggk, rope_m, rope_g)


def _attn_kernel(q_ref, kl_ref, kc_ref, vl_ref, vc_ref, o_ref, s_scr, p_scr, *, n_heads, k_share, v_share, n_lat_batches):
    tq = q_ref.shape[0]
    lane = lax.broadcasted_iota(jnp.int32, (tq, LANES), 1)

    def run(parts):
        chunks, off = [], 0
        for k_ref, _ in parts:
            n = k_ref.shape[0]
            size = min(KV_CHUNK, n)
            for c in range(n // size):
                chunks.append((k_ref, c * size, size, off + c * size))
            off += n
        mx, m, lsum = {}, {}, {}
        outs = [None] * n_heads

        def scores(hd, ci):
            k_ref, r0, size, c0 = chunks[ci]
            kb = hd // k_share
            s = _dot_nt(q_ref[:, hd * LANES:(hd + 1) * LANES], k_ref[r0:r0 + size, kb * LANES:(kb + 1) * LANES])
            s_scr[hd % 2, :, c0:c0 + size] = s
            for j in range(size // LANES):
                blk = s[:, j * LANES:(j + 1) * LANES]
                mx[hd] = blk if hd not in mx else jnp.maximum(mx[hd], blk)

        def values(hd, ci):
            _, _, size, c0 = chunks[ci]
            p = jnp.exp2(s_scr[hd % 2, :, c0:c0 + size] - m[hd])
            for j in range(size // LANES):
                blk = p[:, j * LANES:(j + 1) * LANES]
                lsum[hd] = blk if hd not in lsum else lsum[hd] + blk
            p_scr[hd % 2, :, c0:c0 + size] = p.astype(BF16)

        def product(hd):
            vb = hd // v_share
            out, off = None, 0
            for _, v_ref in parts:
                n = v_ref.shape[0]
                part = _dot(p_scr[hd % 2, :, off:off + n], v_ref[:, vb * LANES:(vb + 1) * LANES])
                out = part if out is None else out + part
                off += n
            return out

        for j in range(n_heads + 1):
            for ci in range(len(chunks)):
                if j < n_heads:
                    scores(j, ci)
                if j >= 1:
                    values(j - 1, ci)
            if j < n_heads:
                m[j] = jnp.max(mx.pop(j), axis=-1, keepdims=True)
            if j >= 1:
                outs[j - 1] = product(j - 1) * (1.0 / jnp.sum(lsum.pop(j - 1), axis=-1, keepdims=True))
        for j in range(n_heads // 2):
            pair = jnp.where(lane < LANES // 2, outs[2 * j], outs[2 * j + 1])
            o_ref[:, j * LANES:(j + 1) * LANES] = pair.astype(o_ref.dtype)

    is_lat = pl.program_id(0) < n_lat_batches

    @pl.when(is_lat)
    def _():
        run([(kc_ref, vc_ref), (kl_ref, vl_ref)])

    @pl.when(jnp.logical_not(is_lat))
    def _():
        run([(kc_ref, vc_ref)])


def _attention(Q, K, V, *, k_share, v_share, ctx_len):
    nb, t, qw = Q.shape
    nbl = nb - 1
    n_heads = qw // LANES
    assert ctx_len == TQ and nbl * ctx_len == t
    kw, vw, ow = K.shape[-1], V.shape[-1], qw // 2
    lat_b = lambda b: jnp.minimum(b, nbl - 1)
    ctx_blk = lambda b, i: jnp.where(b < nbl, b, i)
    return pl.pallas_call(
        functools.partial(_attn_kernel, n_heads=n_heads, k_share=k_share, v_share=v_share, n_lat_batches=nbl),
        out_shape=jax.ShapeDtypeStruct((nb, t, ow), BF16),
        grid=(nb, t // TQ),
        in_specs=[
            pl.BlockSpec((None, TQ, qw), lambda b, i: (b, i, 0)),
            pl.BlockSpec((None, t, kw), lambda b, i: (lat_b(b), 0, 0)),
            pl.BlockSpec((None, ctx_len, kw), lambda b, i: (nbl, ctx_blk(b, i), 0)),
            pl.BlockSpec((None, t, vw), lambda b, i: (lat_b(b), 0, 0)),
            pl.BlockSpec((None, ctx_len, vw), lambda b, i: (nbl, ctx_blk(b, i), 0)),
        ],
        out_specs=pl.BlockSpec((None, TQ, ow), lambda b, i: (b, i, 0)),
        scratch_shapes=[pltpu.VMEM((2, TQ, t + ctx_len), F32), pltpu.VMEM((2, TQ, t + ctx_len), BF16)],
        compiler_params=_cparams(("parallel", "arbitrary")),
        name="attn_k%d_v%d" % (k_share, v_share),
    )(Q, K, K, V, V)


def _even_out_kernel(x_ref, gt_ref, oa_ref, ob_ref, w_ref, y_ref):
    na = oa_ref.shape[1]
    out = _dot(oa_ref[...], w_ref[0:na, :]) + _dot(ob_ref[...], w_ref[na:, :])
    y_ref[...] = x_ref[...] + gt_ref[...] * out


def _even_out(X, gt, Oa, Ob, w_out):
    nb, t, d = X.shape
    tm = TM_OUT
    tile = lambda w: pl.BlockSpec((None, tm, w), lambda b, i: (b, i, 0))
    return pl.pallas_call(
        _even_out_kernel,
        out_shape=jax.ShapeDtypeStruct((nb, t, d), F32),
        grid=(nb, t // tm),
        in_specs=[tile(d), pl.BlockSpec((None, 1, d), lambda b, i: (b, 0, 0)), tile(Oa.shape[-1]), tile(Ob.shape[-1]),
                  pl.BlockSpec(w_out.shape, lambda b, i: (0, 0))],
        out_specs=tile(d),
        compiler_params=_cparams(("parallel", "parallel")),
        name="even_out",
    )(X, gt, Oa, Ob, w_out)


def _ffn_kernel(x_ref, xp_ref, xn_ref, sh_ref, sc_ref, gt_ref, g2_ref, wg_ref, wv_ref, cw_ref, cb_ref, wd_ref,
                y_ref, a_scr, *, seq_len, ctx_len, n_lat_batches):
    tm = x_ref.shape[0]
    text = tm + 2 * SUBLANES
    x = x_ref[...]
    g2, sh, sc = g2_ref[...], sh_ref[...], sc_ref[...]
    xext = jnp.concatenate([xp_ref[...], x, xn_ref[...]], axis=0)
    hext = _norm_mod(xext, g2, sh, sc).astype(BF16)
    has_prev, has_next = _edge_masks(pl.program_id(1) * tm, tm, pl.program_id(0) < n_lat_batches, seq_len, ctx_len)
    mid = slice(SUBLANES, SUBLANES + tm)
    for j in range(D_FF // FF_CHUNK):
        cols = slice(j * FF_CHUNK, (j + 1) * FF_CHUNK)
        gext = _dot(hext, wg_ref[:, cols])
        val = _dot(hext, wv_ref[:, cols])[mid]
        prev = jnp.where(has_prev, pltpu.roll(gext, 1, 0)[mid], 0.0)
        nxt = jnp.where(has_next, pltpu.roll(gext, text - 1, 0)[mid], 0.0)
        conv = prev * cw_ref[0:1, cols] + gext[mid] * cw_ref[1:2, cols] + nxt * cw_ref[2:3, cols] + cb_ref[:, cols]
        a_scr[:, cols] = (_silu(conv) * val).astype(BF16)
    y_ref[...] = x + gt_ref[...] * _dot(a_scr[...], wd_ref[...])


def _ffn(X, sh, sc, gt, g2, wg, wv, cw, cb, wd, *, n_out_batches, n_lat_batches, ctx_len):
    nb, t, d = X.shape
    tm = TM_FFN
    nblk = t // SUBLANES
    per = tm // SUBLANES
    row = pl.BlockSpec((None, 1, d), lambda b, i: (b, 0, 0))
    full = lambda a: pl.BlockSpec(a.shape, lambda b, i: (0,) * a.ndim, pipeline_mode=pl.Buffered(1))
    return pl.pallas_call(
        functools.partial(_ffn_kernel, seq_len=t, ctx_len=ctx_len, n_lat_batches=n_lat_batches),
        out_shape=jax.ShapeDtypeStruct((n_out_batches, t, d), F32),
        grid=(n_out_batches, t // tm),
        in_specs=[
            pl.BlockSpec((None, tm, d), lambda b, i: (b, i, 0)),
            pl.BlockSpec((None, SUBLANES, d), lambda b, i: (b, jnp.maximum(i * per - 1, 0), 0)),
            pl.BlockSpec((None, SUBLANES, d), lambda b, i: (b, jnp.minimum((i + 1) * per, nblk - 1), 0)),
            row, row, row, full(g2), full(wg), full(wv), full(cw), full(cb), full(wd),
        ],
        out_specs=pl.BlockSpec((None, tm, d), lambda b, i: (b, i, 0)),
        scratch_shapes=[pltpu.VMEM((tm, D_FF), BF16)],
        compiler_params=_cparams(("parallel", "parallel")),
        name="conv_ffn",
    )(X, X, X, sh, sc, gt, g2, wg, wv, cw, cb, wd)


def _odd_in_kernel(x_ref, xp_ref, xn_ref, sh_ref, sc_ref, g1_ref, w_ref, gb_ref, cw_ref, cb_ref,
                   qk_ref, v_ref, o_ref, g_ref, *, seq_len, ctx_len, n_lat_batches):
    tm = x_ref.shape[0]
    text = tm + 2 * SUBLANES
    mid = slice(SUBLANES, SUBLANES + tm)
    xext = jnp.concatenate([xp_ref[...], x_ref[...], xn_ref[...]], axis=0)
    hext = _norm_mod(xext, g1_ref[...], sh_ref[...], sc_ref[...]).astype(BF16)
    nqk = qk_ref.shape[1]
    nv = v_ref.shape[1]
    has_prev, has_next = _edge_masks(pl.program_id(1) * tm, tm, pl.program_id(0) < n_lat_batches, seq_len, ctx_len)
    step = 2 * LANES
    for j in range(nqk // step):
        cols = slice(j * step, (j + 1) * step)
        ext = _dot(hext, w_ref[:, cols])
        prev = jnp.where(has_prev, pltpu.roll(ext, 1, 0)[mid], 0.0)
        nxt = jnp.where(has_next, pltpu.roll(ext, text - 1, 0)[mid], 0.0)
        conv = prev * cw_ref[0:1, cols] + ext[mid] * cw_ref[1:2, cols] + nxt * cw_ref[2:3, cols] + cb_ref[:, cols]
        scale = MLSTM_QK ** -0.5 if j >= nqk // (2 * step) else 1.0
        qk_ref[:, cols] = (_silu(conv) * scale).astype(BF16)
    v_ref[...] = _dot(hext, w_ref[:, nqk:nqk + nv])[mid].astype(BF16)
    o_ref[...] = _dot(hext, w_ref[:, nqk + nv:nqk + 2 * nv])[mid]
    g_ref[...] = _dot(hext, w_ref[:, nqk + 2 * nv:])[mid] + gb_ref[...]


def _odd_in(X, sh, sc, g1, w, gate_b, cw, cb, *, ctx_len):
    nb, t, d = X.shape
    tm = TM_OUT
    nblk = t // SUBLANES
    per = tm // SUBLANES
    nqk = 2 * MLSTM_HEADS * MLSTM_QK
    nv = MLSTM_HEADS * MLSTM_V
    row = pl.BlockSpec((None, 1, d), lambda b, i: (b, 0, 0))
    full = lambda a: pl.BlockSpec(a.shape, lambda b, i: (0,) * a.ndim)
    tile = lambda wd: pl.BlockSpec((None, tm, wd), lambda b, i: (b, i, 0))
    return pl.pallas_call(
        functools.partial(_odd_in_kernel, seq_len=t, ctx_len=ctx_len, n_lat_batches=nb - 1),
        out_shape=[jax.ShapeDtypeStruct((nb, t, nqk), BF16), jax.ShapeDtypeStruct((nb, t, nv), BF16),
                   jax.ShapeDtypeStruct((nb, t, nv), F32), jax.ShapeDtypeStruct((nb, t, LANES), F32)],
        grid=(nb, t // tm),
        in_specs=[tile(d),
                  pl.BlockSpec((None, SUBLANES, d), lambda b, i: (b, jnp.maximum(i * per - 1, 0), 0)),
                  pl.BlockSpec((None, SUBLANES, d), lambda b, i: (b, jnp.minimum((i + 1) * per, nblk - 1), 0)),
                  row, row, full(g1), full(w), full(gate_b), full(cw), full(cb)],
        out_specs=[tile(nqk), tile(nv), tile(nv), tile(LANES)],
        compiler_params=_cparams(("parallel", "parallel")),
        name="odd_in",
    )(X, X, X, sh, sc, g1, w, gate_b, cw, cb)


def _log_sigmoid(x):
    return jnp.minimum(x, 0.0) - jnp.log(1.0 + jnp.exp(-jnp.abs(x)))


def _running_max(x, rev):
    n = x.shape[0]
    row = lax.broadcasted_iota(jnp.int32, x.shape, 0)
    k = 1
    while k < n:
        if rev:
            x = jnp.where(row < n - k, jnp.maximum(x, pltpu.roll(x, n - k, 0)), x)
        else:
            x = jnp.where(row >= k, jnp.maximum(x, pltpu.roll(x, k, 0)), x)
        k *= 2
    return x


def _mlstm_kernel(qf_ref, kf_ref, vf_ref, gf_ref, qb_ref, kb_ref, vb_ref, gb_ref, hf_ref, hb_ref, cn_scr, m_scr):
    @pl.when(pl.program_id(1) == 0)
    def _():
        cn_scr[...] = jnp.zeros_like(cn_scr)
        m_scr[...] = jnp.zeros_like(m_scr)

    dirs = (0, 1)
    q_refs, k_refs, v_refs = (qf_ref, qb_ref), (kf_ref, kb_ref), (vf_ref, vb_ref)
    g_refs, h_refs = (gf_ref, gb_ref), (hf_ref, hb_ref)
    L = qf_ref.shape[0]
    nh = MLSTM_HEADS
    i_off = (0, 2 * nh)
    last = (L - 1, 0)
    r_i = lax.broadcasted_iota(jnp.int32, (L, L), 0)
    c_i = lax.broadcasted_iota(jnp.int32, (L, L), 1)
    tri = (c_i <= r_i, c_i >= r_i)
    g = [g_refs[d][...] for d in dirs]
    b_f = [jnp.dot(tri[d].astype(F32), _log_sigmoid(g[d]), precision=lax.Precision.HIGHEST,
                   preferred_element_type=F32) for d in dirs]
    b = [pltpu.roll(b_f[d], LANES - nh, 1) for d in dirs]
    u = [g[d] - b[d] for d in dirs]
    m_prev = [m_scr[d] for d in dirs]
    cmax = [_running_max(u[d], d == 1) for d in dirs]
    mm = [jnp.maximum(m_prev[d], cmax[d]) for d in dirs]
    em = [jnp.exp(-(b[d] + mm[d])) for d in dirs]
    b_last = [b[d][last[d]:last[d] + 1, :] for d in dirs]
    a = [b_last[d] - b[d] + g[d] for d in dirs]
    m_new = [jnp.maximum(b_last[d] + m_prev[d], jnp.max(a[d], axis=0, keepdims=True)) for d in dirs]
    decay = [jnp.exp(b_last[d] + m_prev[d] - m_new[d]) for d in dirs]
    w = [jnp.exp(a[d] - m_new[d]) for d in dirs]
    for d in dirs:
        m_scr[d] = m_new[d]
    u_t = [u[d].T for d in dirs]
    w_t = [w[d].T for d in dirs]
    ones = jnp.ones((L, LANES), BF16)
    row = lax.broadcasted_iota(jnp.int32, (LANES, L), 0)
    for pair in range(nh // 2):
        pc = slice(pair * LANES, (pair + 1) * LANES)
        qp = [q_refs[d][:, pc] for d in dirs]
        kt = [k_refs[d][:, pc].astype(F32).T for d in dirs]
        ch = [(d, 2 * pair + half, i_off[d] + 2 * pair + half, half) for half in range(2) for d in dirs]
        hc = lambda hd: slice(hd * LANES, (hd + 1) * LANES)
        kt_h = [jnp.where((row < MLSTM_QK) if half == 0 else (row >= MLSTM_QK), kt[d], 0.0)
                for d, hd, col, half in ch]
        v1 = [jnp.concatenate([v_refs[d][:, hc(hd)], ones], axis=1) for d, hd, col, half in ch]
        mm_b = [jnp.broadcast_to(mm[d][:, col:col + 1], (L, LANES)) for d, hd, col, half in ch]
        em_b = [jnp.broadcast_to(em[d][:, col:col + 1], (L, LANES)) for d, hd, col, half in ch]
        dw = [jnp.where(tri[d], jnp.exp(u_t[d][col:col + 1, :] - mm_b[i]), 0.0) for i, (d, hd, col, half) in enumerate(ch)]
        s = [_dot(qp[d], kt_h[i].astype(BF16)) * dw[i] for i, (d, hd, col, half) in enumerate(ch)]
        cn = [cn_scr[d, hd] for d, hd, col, half in ch]
        qc = [_dot(qp[d], cn[i].astype(BF16)) for i, (d, hd, col, half) in enumerate(ch)]
        sv = [_dot(s[i].astype(BF16), v1[i]) for i in range(len(ch))]
        w_inter = [jnp.exp(m_prev[d][:, col:col + 1] - mm_b[i]) for i, (d, hd, col, half) in enumerate(ch)]
        num = [w_inter[i] * qc[i][:, :LANES] + sv[i][:, :LANES] for i in range(len(ch))]
        den = [w_inter[i] * qc[i][:, LANES:] + sv[i][:, LANES:] for i in range(len(ch))]
        for i, (d, hd, col, half) in enumerate(ch):
            h_refs[d][:, hc(hd)] = num[i] * (1.0 / jnp.maximum(jnp.abs(den[i]), em_b[i]))
        ktw = [(kt_h[i] * w_t[d][col:col + 1, :]).astype(BF16) for i, (d, hd, col, half) in enumerate(ch)]
        upd = [_dot(ktw[i], v1[i]) for i in range(len(ch))]
        for i, (d, hd, col, half) in enumerate(ch):
            cn_scr[d, hd] = decay[d][:, col:col + 1] * cn[i] + upd[i]


def _mlstm(QK, V, G, *, ctx_len):
    nb, t, nqk = QK.shape
    nbl = nb - 1
    L = ML_CHUNK
    ncc, nlc = ctx_len // L, t // L
    nq = nqk // 2
    nv = V.shape[-1]

    def spec(width, col, rev):
        def index(b, c):
            cc, lc = (ncc - 1 - c, nlc - 1 - (c - ncc)) if rev else (c, c - ncc)
            is_ctx = c < ncc
            return jnp.where(is_ctx, nbl, b), jnp.where(is_ctx, b * ncc + cc, lc), col
        return pl.BlockSpec((None, L, width), index)

    ins = lambda rev: [spec(nq, 0, rev), spec(nq, 1, rev), spec(nv, 0, rev), spec(LANES, 0, rev)]
    out = jax.ShapeDtypeStruct((nb, t, nv), F32)
    return pl.pallas_call(
        _mlstm_kernel,
        out_shape=[out, out],
        grid=(nbl, ncc + nlc),
        in_specs=ins(False) + ins(True),
        out_specs=[spec(nv, 0, False), spec(nv, 0, True)],
        scratch_shapes=[pltpu.VMEM((2, MLSTM_HEADS, LANES, MLSTM_V + LANES), F32),
                        pltpu.VMEM((2, 1, LANES), F32)],
        compiler_params=_cparams(("parallel", "arbitrary")),
        name="mlstm",
    )(QK, QK, V, G, QK, QK, V, G)


def _readout_kernel(x_ref, gt_ref, hf_ref, hb_ref, o_ref, og_ref, w_ref, y_ref, a_scr):
    for hd in range(MLSTM_HEADS):
        cols = slice(hd * LANES, (hd + 1) * LANES)
        hsum = hf_ref[:, cols] + hb_ref[:, cols]
        hn = _rms(hsum, og_ref[:, cols], 1.0 / MLSTM_V)
        gate = 1.0 / (1.0 + jnp.exp(-o_ref[:, cols]))
        a_scr[:, cols] = (hn * gate).astype(BF16)
    y_ref[...] = x_ref[...] + gt_ref[...] * _dot(a_scr[...], w_ref[...])


def _readout(X, gt, Hf, Hb, O, out_g, w_out, *, n_out_batches):
    nb, t, d = X.shape
    tm = TM_OUT
    n = Hf.shape[-1]
    tile = lambda w: pl.BlockSpec((None, tm, w), lambda b, i: (b, i, 0))
    full = lambda a: pl.BlockSpec(a.shape, lambda b, i: (0,) * a.ndim)
    return pl.pallas_call(
        _readout_kernel,
        out_shape=jax.ShapeDtypeStruct((n_out_batches, t, d), F32),
        grid=(n_out_batches, t // tm),
        in_specs=[tile(d), pl.BlockSpec((None, 1, d), lambda b, i: (b, 0, 0)), tile(n), tile(n), tile(n),
                  full(out_g), full(w_out)],
        out_specs=tile(d),
        scratch_shapes=[pltpu.VMEM((tm, n), BF16)],
        compiler_params=_cparams(("parallel", "parallel")),
        name="readout",
    )(X, gt, Hf, Hb, O, out_g, w_out)


def kernel(x, c, ctx, c_ctx, ada_w, ada_b, norm1_g, norm2_g, ffn_w_up, ffn_conv_w, ffn_conv_b, ffn_w_down,
           att_w_in, mla_qa_g, mla_w_qb, mla_kva_g, mla_w_kvb, mla_q_g, mla_k_g, gqa_q_g, gqa_k_g, att_w_out,
           ml_w_in, ml_conv_w, ml_conv_b, ml_gate_b, ml_out_g, ml_w_out):
    B, T, D = x.shape
    ctx_len = ctx.shape[1]
    depth = ada_w.shape[0]
    assert B * ctx_len == T, "context sequences must tile one latent-length row of the token array"
    NB = B + 1

    X = jnp.concatenate([x, ctx.reshape(1, T, D)], axis=0)

    rows = -(-NB // SUBLANES) * SUBLANES
    cvec = jnp.zeros((rows, D), F32).at[:B].set(c).at[B].set(c_ctx)
    mod = _adaln(cvec, ada_w, ada_b)
    mod = mod.reshape(depth, rows, 6, 1, D).transpose(0, 2, 1, 3, 4)

    rope_m = _rope_tables(T, MLA_ROPE, MLA_NOPE)
    rope_g = _rope_tables(T, GQA_DIM, 0)

    for layer in range(depth):
        last = layer == depth - 1
        j = layer // 2
        sh1, sc1, gt1, sh2, sc2, gt2 = (mod[layer, i] for i in range(6))
        g1 = norm1_g[layer][None, :]
        g2 = norm2_g[layer][None, :]
        if layer % 2 == 0:
            ew = _even_weights(att_w_in[j], mla_w_qb[j], mla_w_kvb[j], mla_q_g[j], mla_k_g[j],
                               gqa_q_g[j], gqa_k_g[j])
            qm, km, vm, qg, kg, vg = _even_in(X, sh1, sc1, g1, ew, mla_qa_g[j][None, :], mla_kva_g[j][None, :],
                                              rope_m, rope_g)
            oa = _attention(qm, km, vm, k_share=1, v_share=2, ctx_len=ctx_len)
            ob = _attention(qg, kg, vg, k_share=GQA_HEADS // GQA_KV_HEADS, v_share=GQA_HEADS // GQA_KV_HEADS,
                            ctx_len=ctx_len)
            X = _even_out(X, gt1, oa, ob, att_w_out[j].astype(BF16))
        else:
            gate_b = _pad_cols(ml_gate_b[j][None, :], LANES)
            w_in = jnp.pad(ml_w_in[j], ((0, 0), (0, LANES - 4 * MLSTM_HEADS))).astype(BF16)
            qk, v, o, gates = _odd_in(X, sh1, sc1, g1, w_in, gate_b, ml_conv_w[j], ml_conv_b[j][None, :],
                                      ctx_len=ctx_len)
            hf, hb = _mlstm(qk, v, gates, ctx_len=ctx_len)
            X = _readout(X, gt1, hf, hb, o, ml_out_g[j].reshape(1, -1), ml_w_out[j].astype(BF16),
                         n_out_batches=B if last else NB)
        w_up = ffn_w_up[layer].astype(BF16)
        X = _ffn(X, sh2, sc2, gt2, g2, w_up[:, :D_FF], w_up[:, D_FF:], ffn_conv_w[layer],
                 ffn_conv_b[layer][None, :], ffn_w_down[layer].astype(BF16),
                 n_out_batches=B if last else NB, n_lat_batches=B, ctx_len=ctx_len)
    return X[:B]
```

```python
import functools

import numpy as np
import jax
import jax.numpy as jnp
from jax import lax
from jax.experimental import pallas as pl
from jax.experimental.pallas import tpu as pltpu

F32 = jnp.float32
BF16 = jnp.bfloat16

EPS = 1e-6
ROPE_THETA = 10000.0
GRID_W = 64
LOG2E = 1.4426950408889634

MLA_HEADS = 8
MLA_Q_RANK = 384
MLA_KV_RANK = 256
MLA_NOPE = 64
MLA_ROPE = 32
MLA_V = 64
MLA_QK = MLA_NOPE + MLA_ROPE
GQA_HEADS = 8
GQA_KV_HEADS = 2
GQA_DIM = 64
MLSTM_HEADS = 8
MLSTM_QK = 64
MLSTM_V = 128
D_FF = 2816

LANES = 128
SUBLANES = 8
VMEM_LIMIT = 56 * 1024 * 1024

TM_IN = 256
TM_ODD = 512
TM_FFN = 512
FF_CHUNK = 256
TQ = 256
KV_CHUNK = 512
ML_CHUNK = 128


def _cparams(sem):
    return pltpu.CompilerParams(dimension_semantics=sem, vmem_limit_bytes=VMEM_LIMIT)


def _dot(a, b):
    return jnp.dot(a, b, preferred_element_type=F32)


def _dot_nt(a, b):
    return lax.dot_general(a, b, (((1,), (1,)), ((), ())), preferred_element_type=F32)


def _rms(x, gain_row, inv_n):
    ss = jnp.sum(x * x, axis=-1, keepdims=True) * inv_n
    return x * lax.rsqrt(ss + EPS) * gain_row


def _norm_mod(x, g_row, sh_row, sc_row):
    ms = jnp.mean(x * x, axis=-1, keepdims=True)
    return x * lax.rsqrt(ms + EPS) * (g_row * (1.0 + sc_row)) + sh_row


def _silu(x):
    return x * (1.0 / (1.0 + jnp.exp(-x)))


def _edge_masks(row0, n_rows, is_latent, seq_len, ctx_len):
    assert seq_len & (seq_len - 1) == 0 and ctx_len & (ctx_len - 1) == 0, "sequence lengths must be powers of two"
    last = jnp.where(is_latent, seq_len - 1, ctx_len - 1)
    pos = (row0 + lax.broadcasted_iota(jnp.int32, (n_rows, 1), 0)) & last
    return pos != 0, pos != last


def _lockstep(chains):
    chains = list(chains)
    while chains:
        for c in list(chains):
            try:
                next(c)
            except StopIteration:
                chains.remove(c)


class _Tokens:
    def __init__(self, lat, ctx=None):
        self.lat, self.ctx = lat, ctx
        self.nb = lat.shape[0] + (0 if ctx is None else 1)
        self.t, self.d = lat.shape[1], lat.shape[2]
        self.n_refs = 1 if ctx is None else 2

    def specs(self, rows, blk):
        if self.ctx is None:
            return [self.lat], [pl.BlockSpec((None, rows, self.d), lambda b, i: (b, blk(i), 0))]
        nbl = self.nb - 1
        lat = pl.BlockSpec((None, rows, self.d), lambda b, i: (jnp.minimum(b, nbl - 1), jnp.where(b < nbl, blk(i), 0), 0))
        ctx = pl.BlockSpec((None, rows, self.d), lambda b, i: (0, jnp.where(b < nbl, 0, blk(i)), 0))
        return [self.lat, self.ctx], [lat, ctx]


def _tok(refs, is_latent):
    if len(refs) == 1:
        return refs[0][...]
    return jnp.where(is_latent, refs[0][...], refs[1][...])


def _adaln_kernel(c_ref, w_ref, b_ref, o_ref):
    c = _silu(c_ref[...]).astype(BF16)
    o_ref[...] = _dot(c, w_ref[...].astype(BF16)) + b_ref[...]


def _adaln(cvec, ada_w, ada_b):
    depth, d, n = ada_w.shape
    rows = cvec.shape[0]
    tn = 1536
    return pl.pallas_call(
        _adaln_kernel,
        out_shape=jax.ShapeDtypeStruct((depth, rows, n), F32),
        grid=(depth, n // tn),
        in_specs=[
            pl.BlockSpec((rows, d), lambda l, j: (0, 0)),
            pl.BlockSpec((None, d, tn), lambda l, j: (l, 0, j)),
            pl.BlockSpec((None, 1, tn), lambda l, j: (l, 0, j)),
        ],
        out_specs=pl.BlockSpec((None, rows, tn), lambda l, j: (l, 0, j)),
        compiler_params=_cparams(("arbitrary", "arbitrary")),
        name="adaln",
    )(cvec, ada_w, ada_b.reshape(depth, 1, n))


def _rope(y, cos, sa, sb, shift):
    return y * cos + pltpu.roll(y, LANES - shift, 1) * sa + pltpu.roll(y, shift, 1) * sb


def _pair_sumsq(y):
    n = y.shape[1]
    r = lax.broadcasted_iota(jnp.int32, (n, n), 0) // LANES
    c = lax.broadcasted_iota(jnp.int32, (n, n), 1) // LANES
    return _dot((y * y).astype(BF16), jnp.where(r == c, 1.0, 0.0).astype(BF16))


def _rope_pair(y, tabs, shift):
    cos, sa, sb = tabs
    return jnp.concatenate([_rope(y[:, :LANES], cos, sa, sb, shift), _rope(y[:, LANES:], cos, sa, sb, shift)], axis=1)


def _even_in_kernel(*refs, n_tok_refs, n_lat_batches):
    x_refs, refs = refs[:n_tok_refs], refs[n_tok_refs:]
    (sh_ref, sc_ref, g1_ref, win_ref, gqa_ref, wqb_ref, gkva_ref, wk_ref, wv_ref,
     gmq_ref, gmk_ref, ggq_ref, ggk_ref, rm_ref, rg_ref,
     qm_ref, km_ref, vm_ref, qg_ref, kg_ref, vg_ref) = refs
    x = _tok(x_refs, pl.program_id(0) < n_lat_batches)
    h = _norm_mod(x, g1_ref[...], sh_ref[...], sc_ref[...]).astype(BF16)
    rope_m = (rm_ref[0], rm_ref[1], rm_ref[2])
    rope_g = (rg_ref[0], rg_ref[1], rg_ref[2])
    pw = 2 * LANES
    o_ckv = MLA_Q_RANK
    o_kr = o_ckv + MLA_KV_RANK
    o_gq = o_kr + LANES
    o_gk = o_gq + GQA_HEADS * LANES
    o_gv = o_gk + GQA_KV_HEADS * LANES
    two = lambda a: jnp.concatenate([a, a], axis=1)

    low = _dot(h, win_ref[:, 0:o_gq])
    cqn = _rms(low[:, 0:o_ckv], gqa_ref[...], 1.0 / MLA_Q_RANK).astype(BF16)
    ckvn = _rms(low[:, o_ckv:o_kr], gkva_ref[...], 1.0 / MLA_KV_RANK).astype(BF16)
    krp = low[:, o_kr:o_gq]
    gmk = gmk_ref[...]
    kr_rot = two(_rope(krp * gmk[:, :LANES], *rope_m, MLA_ROPE // 4))
    kr_ss = _pair_sumsq(two(krp))
    vm_ref[...] = _dot(ckvn, wv_ref[...]).astype(BF16)
    vg_ref[...] = _dot(h, win_ref[:, o_gv:o_gv + GQA_KV_HEADS * LANES]).astype(BF16)

    def normed_roped(lhs, w_ref, c0, gain_ref, inv_n, tabs, shift, out_ref, p):
        blk = _dot(lhs, w_ref[:, c0 + p * pw:c0 + (p + 1) * pw])
        yield
        ss = _pair_sumsq(blk)
        yield
        y = blk * lax.rsqrt(ss * inv_n + EPS) * gain_ref[...]
        yield
        out_ref[:, p * pw:(p + 1) * pw] = _rope_pair(y, tabs, shift).astype(BF16)

    def mla_key(p):
        blk = _dot(ckvn, wk_ref[:, p * pw:(p + 1) * pw])
        yield
        ss = _pair_sumsq(blk)
        yield
        r = lax.rsqrt((ss + kr_ss) * (1.0 / MLA_QK) + EPS)
        yield
        km_ref[:, p * pw:(p + 1) * pw] = (r * (blk * gmk + kr_rot)).astype(BF16)

    for p in range(MLA_HEADS // 2):
        chains = [normed_roped(cqn, wqb_ref, 0, gmq_ref, 1.0 / MLA_QK, rope_m, MLA_ROPE // 4, qm_ref, p),
                  mla_key(p),
                  normed_roped(h, win_ref, o_gq, ggq_ref, 1.0 / GQA_DIM, rope_g, GQA_DIM // 4, qg_ref, p)]
        if p < GQA_KV_HEADS // 2:
            chains.append(normed_roped(h, win_ref, o_gk, ggk_ref, 1.0 / GQA_DIM, rope_g, GQA_DIM // 4, kg_ref, p))
        _lockstep(chains)


def _pad_cols(w, width):
    return jnp.pad(w, ((0, 0), (0, width - w.shape[1])))


def _even_weights(w_in, w_qb, w_kvb, g_mq, g_mk, g_gq, g_gk):
    d = w_in.shape[0]
    s0 = MLA_Q_RANK
    s1 = s0 + MLA_KV_RANK
    s2 = s1 + MLA_ROPE
    s3 = s2 + GQA_HEADS * GQA_DIM
    s4 = s3 + GQA_KV_HEADS * GQA_DIM
    cq, ckv, kr = w_in[:, :s0], w_in[:, s0:s1], w_in[:, s1:s2]
    gq, gk, gv = w_in[:, s2:s3], w_in[:, s3:s4], w_in[:, s4:]
    kr_p = jnp.pad(kr, ((0, 0), (MLA_NOPE, LANES - MLA_QK)))
    gq_p = jnp.pad(gq.reshape(d, GQA_HEADS, GQA_DIM), ((0, 0), (0, 0), (0, LANES - GQA_DIM))).reshape(d, -1)
    gk_p = jnp.pad(gk.reshape(d, GQA_KV_HEADS, GQA_DIM), ((0, 0), (0, 0), (0, LANES - GQA_DIM))).reshape(d, -1)
    gv3 = gv.reshape(d, GQA_KV_HEADS, GQA_DIM)
    gv_p = jnp.concatenate([gv3, gv3], axis=-1).reshape(d, -1)
    win = jnp.concatenate([cq, ckv, kr_p, gq_p, gk_p, gv_p], axis=1).astype(BF16)
    wqb = jnp.pad(w_qb.reshape(MLA_Q_RANK, MLA_HEADS, MLA_QK),
                  ((0, 0), (0, 0), (0, LANES - MLA_QK))).reshape(MLA_Q_RANK, -1).astype(BF16)
    kv3 = w_kvb.reshape(MLA_KV_RANK, MLA_HEADS, MLA_NOPE + MLA_V)
    wk = jnp.pad(kv3[:, :, :MLA_NOPE], ((0, 0), (0, 0), (0, LANES - MLA_NOPE))).reshape(MLA_KV_RANK, -1).astype(BF16)
    wv = kv3[:, :, MLA_NOPE:].reshape(MLA_KV_RANK, -1).astype(BF16)
    pair = lambda g: jnp.tile(_pad_cols(g[None, :], LANES), (1, 2))
    gmq = pair(g_mq * (MLA_QK ** -0.5 * LOG2E))
    gmk = pair(g_mk)
    ggq = pair(g_gq * (GQA_DIM ** -0.5 * LOG2E))
    ggk = pair(g_gk)
    return win, wqb, wk, wv, gmq, gmk, ggq, ggk


def _rope_tables(n_tokens, rot_dim, offset):
    rows = n_tokens // GRID_W
    row = np.repeat(np.arange(rows), GRID_W).astype(np.float32)
    col = np.tile(np.arange(GRID_W), rows).astype(np.float32)
    n_freq = rot_dim // 4
    inv = (np.float32(ROPE_THETA) ** (-np.arange(n_freq, dtype=np.float32) / np.float32(n_freq))).astype(np.float32)
    a_r, a_c = row[:, None] * inv, col[:, None] * inv
    ang = np.concatenate([a_r, a_r, a_c, a_c], axis=-1)
    cos, sin = np.cos(ang), np.sin(ang)
    quarter = (np.arange(rot_dim) // n_freq) % 2
    sa = np.where(quarter == 0, -sin, 0.0)
    sb = np.where(quarter == 1, sin, 0.0)
    pad = ((0, 0), (offset, LANES - offset - rot_dim))
    lat = np.stack([np.pad(cos, pad, constant_values=1.0), np.pad(sa, pad), np.pad(sb, pad)])
    ident = np.stack([np.ones((n_tokens, LANES)), np.zeros((n_tokens, LANES)), np.zeros((n_tokens, LANES))])
    return jnp.asarray(np.stack([lat, ident]), F32)


def _even_in(tok, sh, sc, g1, ew, gqa, gkva, rope_m, rope_g):
    nb, t, d = tok.nb, tok.t, tok.d
    win, wqb, wk, wv, gmq, gmk, ggq, ggk = ew
    tm = TM_IN
    full = lambda a: pl.BlockSpec(a.shape, lambda b, i: (0,) * a.ndim)
    row = pl.BlockSpec((None, 1, d), lambda b, i: (b, 0, 0))
    rope_spec = pl.BlockSpec((None, 3, tm, LANES), lambda b, i: (jnp.where(b == nb - 1, 1, 0), 0, i, 0))
    x_args, x_specs = tok.specs(tm, lambda i: i)

    def out(width):
        return jax.ShapeDtypeStruct((nb, t, width), BF16), pl.BlockSpec((None, tm, width), lambda b, i: (b, i, 0))

    outs = [out(MLA_HEADS * LANES), out(MLA_HEADS * LANES), out(MLA_HEADS * MLA_V),
            out(GQA_HEADS * LANES), out(GQA_KV_HEADS * LANES), out(GQA_KV_HEADS * LANES)]
    return pl.pallas_call(
        functools.partial(_even_in_kernel, n_tok_refs=tok.n_refs, n_lat_batches=nb - 1),
        out_shape=[o[0] for o in outs],
        grid=(nb, t // tm),
        in_specs=x_specs + [row, row, full(g1), full(win), full(gqa),
                            full(wqb), full(gkva), full(wk), full(wv), full(gmq), full(gmk), full(ggq), full(ggk),
                            rope_spec, rope_spec],
        out_specs=[o[1] for o in outs],
        compiler_params=_cparams(("parallel", "parallel")),
        name="even_in",
    )(*x_args, sh, sc, g1, win, gqa, wqb, gkva, wk, wv, gmq, gmk, ggq, ggk, rope_m, rope_g)


def _attn_kernel(q_ref, kl_ref, kc_ref, vl_ref, vc_ref, o_ref, s_scr, p_scr, *, n_heads, k_share, v_share, n_lat_batches):
    tq = q_ref.shape[0]
    lane = lax.broadcasted_iota(jnp.int32, (tq, LANES), 1)

    def run(parts):
        chunks, off = [], 0
        for k_ref, _ in parts:
            n = k_ref.shape[0]
            size = min(KV_CHUNK, n)
            for c in range(n // size):
                chunks.append((k_ref, c * size, size, off + c * size))
            off += n
        mx, m, lsum = {}, {}, {}
        outs = [None] * n_heads

        def scores(hd, ci):
            k_ref, r0, size, c0 = chunks[ci]
            kb = hd // k_share
            s = _dot_nt(q_ref[:, hd * LANES:(hd + 1) * LANES], k_ref[r0:r0 + size, kb * LANES:(kb + 1) * LANES])
            s_scr[hd % 2, :, c0:c0 + size] = s
            for j in range(size // LANES):
                blk = s[:, j * LANES:(j + 1) * LANES]
                mx[hd] = blk if hd not in mx else jnp.maximum(mx[hd], blk)

        def values(hd, ci):
            _, _, size, c0 = chunks[ci]
            p = jnp.exp2(s_scr[hd % 2, :, c0:c0 + size] - m[hd])
            for j in range(size // LANES):
                blk = p[:, j * LANES:(j + 1) * LANES]
                lsum[hd] = blk if hd not in lsum else lsum[hd] + blk
            p_scr[hd % 2, :, c0:c0 + size] = p.astype(BF16)

        def product(hd):
            vb = hd // v_share
            out, off = None, 0
            for _, v_ref in parts:
                n = v_ref.shape[0]
                part = _dot(p_scr[hd % 2, :, off:off + n], v_ref[:, vb * LANES:(vb + 1) * LANES])
                out = part if out is None else out + part
                off += n
            return out

        for j in range(n_heads + 1):
            for ci in range(len(chunks)):
                if j < n_heads:
                    scores(j, ci)
                if j >= 1:
                    values(j - 1, ci)
            if j < n_heads:
                m[j] = jnp.max(mx.pop(j), axis=-1, keepdims=True)
            if j >= 1:
                outs[j - 1] = product(j - 1) * (1.0 / jnp.sum(lsum.pop(j - 1), axis=-1, keepdims=True))
        for j in range(n_heads // 2):
            pair = jnp.where(lane < LANES // 2, outs[2 * j], outs[2 * j + 1])
            o_ref[:, j * LANES:(j + 1) * LANES] = pair.astype(o_ref.dtype)

    is_lat = pl.program_id(0) < n_lat_batches

    @pl.when(is_lat)
    def _():
        run([(kc_ref, vc_ref), (kl_ref, vl_ref)])

    @pl.when(jnp.logical_not(is_lat))
    def _():
        run([(kc_ref, vc_ref)])


def _attention(Q, K, V, *, k_share, v_share, ctx_len):
    nb, t, qw = Q.shape
    nbl = nb - 1
    n_heads = qw // LANES
    assert ctx_len == TQ and nbl * ctx_len == t
    kw, vw, ow = K.shape[-1], V.shape[-1], qw // 2
    lat_b = lambda b: jnp.minimum(b, nbl - 1)
    ctx_blk = lambda b, i: jnp.where(b < nbl, b, i)
    return pl.pallas_call(
        functools.partial(_attn_kernel, n_heads=n_heads, k_share=k_share, v_share=v_share, n_lat_batches=nbl),
        out_shape=jax.ShapeDtypeStruct((nb, t, ow), BF16),
        grid=(nb, t // TQ),
        in_specs=[
            pl.BlockSpec((None, TQ, qw), lambda b, i: (b, i, 0)),
            pl.BlockSpec((None, t, kw), lambda b, i: (lat_b(b), 0, 0)),
            pl.BlockSpec((None, ctx_len, kw), lambda b, i: (nbl, ctx_blk(b, i), 0)),
            pl.BlockSpec((None, t, vw), lambda b, i: (lat_b(b), 0, 0)),
            pl.BlockSpec((None, ctx_len, vw), lambda b, i: (nbl, ctx_blk(b, i), 0)),
        ],
        out_specs=pl.BlockSpec((None, TQ, ow), lambda b, i: (b, i, 0)),
        scratch_shapes=[pltpu.VMEM((2, TQ, t + ctx_len), F32), pltpu.VMEM((2, TQ, t + ctx_len), BF16)],
        compiler_params=_cparams(("parallel", "arbitrary")),
        name="attn_k%d_v%d" % (k_share, v_share),
    )(Q, K, K, V, V)


def _ext_rows(main, prev, nxt):
    hp = prev.shape[0]
    return jnp.concatenate([prev.astype(F32)[hp - SUBLANES:hp], main.astype(F32), nxt.astype(F32)[0:SUBLANES]], axis=0)


def _mix_ffn_kernel(*refs, mixer, n_tok_refs, seq_len, ctx_len, n_lat_batches):
    is_latent = pl.program_id(0) < n_lat_batches
    x_parts = [_tok(refs[k * n_tok_refs:(k + 1) * n_tok_refs], is_latent) for k in range(3)]
    refs = refs[3 * n_tok_refs:]
    ext = lambda trio: _ext_rows(*(r[...] for r in trio))
    if mixer == "attn":
        oa_refs, ob_refs, refs = refs[:3], refs[3:6], refs[6:]
    else:
        hf_refs, hb_refs, o_refs, og_ref, refs = refs[:3], refs[3:6], refs[6:9], refs[9], refs[10:]
    (wo_ref, gt1_ref, sh_ref, sc_ref, gt2_ref, g2_ref, wu_ref, cw_ref, cb_ref, wd_ref, y_ref, a_scr) = refs[:12]
    tm = y_ref.shape[0]
    text = tm + 2 * SUBLANES
    mid = slice(SUBLANES, SUBLANES + tm)

    if mixer == "attn":
        oa = ext(oa_refs).astype(BF16)
        ob = ext(ob_refs).astype(BF16)
        na = oa.shape[1]
        mix = _dot(oa, wo_ref[0:na, :]) + _dot(ob, wo_ref[na:, :])
    else:
        m_scr = refs[12]
        hf, hb, og = ext(hf_refs), ext(hb_refs), ext(o_refs)
        for hd in range(MLSTM_HEADS):
            cols = slice(hd * LANES, (hd + 1) * LANES)
            hn = _rms(hf[:, cols] + hb[:, cols], og_ref[:, cols], 1.0 / MLSTM_V)
            m_scr[:, cols] = (hn * (1.0 / (1.0 + jnp.exp(-og[:, cols])))).astype(BF16)
        mix = _dot(m_scr[...], wo_ref[...])
    xext = _ext_rows(*x_parts) + gt1_ref[...] * mix

    hext = _norm_mod(xext, g2_ref[...], sh_ref[...], sc_ref[...]).astype(BF16)
    has_prev, has_next = _edge_masks(pl.program_id(1) * tm, tm, is_latent, seq_len, ctx_len)
    for j in range(D_FF // FF_CHUNK):
        cols = slice(j * FF_CHUNK, (j + 1) * FF_CHUNK)
        gext = _dot(hext, wu_ref[:, cols])
        val = _dot(hext, wu_ref[:, D_FF + j * FF_CHUNK:D_FF + (j + 1) * FF_CHUNK])[mid]
        prev = jnp.where(has_prev, pltpu.roll(gext, 1, 0)[mid], 0.0)
        nxt = jnp.where(has_next, pltpu.roll(gext, text - 1, 0)[mid], 0.0)
        conv = prev * cw_ref[0:1, cols] + gext[mid] * cw_ref[1:2, cols] + nxt * cw_ref[2:3, cols] + cb_ref[:, cols]
        a_scr[:, cols] = (_silu(conv) * val).astype(BF16)
    y_ref[...] = xext[mid] + gt2_ref[...] * _dot(a_scr[...], wd_ref[...])


def _halo_blocks(t, tm, halo):
    per, nblk = tm // halo, t // halo
    return [(tm, lambda i: i), (halo, lambda i: jnp.maximum(i * per - 1, 0)),
            (halo, lambda i: jnp.minimum((i + 1) * per, nblk - 1))]


def _mix_ffn(mixer, tok, mix_in, out_g, w_out, gt1, sh, sc, gt2, g2, w_up, cw, cb, wd, *,
             n_out_batches, n_lat_batches, ctx_len):
    t, d = tok.t, tok.d
    tm = TM_FFN
    row = pl.BlockSpec((None, 1, d), lambda b, i: (b, 0, 0))
    full = lambda a: pl.BlockSpec(a.shape, lambda b, i: (0,) * a.ndim, pipeline_mode=pl.Buffered(1))
    args, specs = [], []
    for rows, blk in _halo_blocks(t, tm, SUBLANES):
        a_, s_ = tok.specs(rows, blk)
        args += a_
        specs += s_
    for a in mix_in:
        halo = SUBLANES * (4 // a.dtype.itemsize)
        for rows, blk in _halo_blocks(t, tm, halo):
            args.append(a)
            specs.append(pl.BlockSpec((None, rows, a.shape[-1]), lambda b, i, blk=blk: (b, blk(i), 0)))
    if mixer == "mlstm":
        args.append(out_g)
        specs.append(full(out_g))
    consts = [w_out, gt1, sh, sc, gt2, g2, w_up, cw, cb, wd]
    args += consts
    specs += [full(w_out), row, row, row, row] + [full(a) for a in consts[5:]]
    return pl.pallas_call(
        functools.partial(_mix_ffn_kernel, mixer=mixer, n_tok_refs=tok.n_refs, seq_len=t, ctx_len=ctx_len,
                          n_lat_batches=n_lat_batches),
        out_shape=jax.ShapeDtypeStruct((n_out_batches, t, d), F32),
        grid=(n_out_batches, t // tm),
        in_specs=specs,
        out_specs=pl.BlockSpec((None, tm, d), lambda b, i: (b, i, 0)),
        scratch_shapes=[pltpu.VMEM((tm, D_FF), BF16)]
        + ([pltpu.VMEM((tm + 2 * SUBLANES, d), BF16)] if mixer == "mlstm" else []),
        compiler_params=_cparams(("parallel", "parallel")),
        name=mixer + "_ffn",
    )(*args)


def _odd_in_kernel(x_ref, xp_ref, xn_ref, sh_ref, sc_ref, g1_ref, w_ref, gb_ref, cw_ref, cb_ref,
                   qk_ref, v_ref, o_ref, g_ref, *, seq_len, ctx_len, n_lat_batches):
    tm = x_ref.shape[0]
    text = tm + 2 * SUBLANES
    mid = slice(SUBLANES, SUBLANES + tm)
    xext = jnp.concatenate([xp_ref[...], x_ref[...], xn_ref[...]], axis=0)
    hext = _norm_mod(xext, g1_ref[...], sh_ref[...], sc_ref[...]).astype(BF16)
    nqk = qk_ref.shape[1]
    nv = v_ref.shape[1]
    has_prev, has_next = _edge_masks(pl.program_id(1) * tm, tm, pl.program_id(0) < n_lat_batches, seq_len, ctx_len)
    step = 2 * LANES
    for j in range(nqk // step):
        cols = slice(j * step, (j + 1) * step)
        ext = _dot(hext, w_ref[:, cols])
        prev = jnp.where(has_prev, pltpu.roll(ext, 1, 0)[mid], 0.0)
        nxt = jnp.where(has_next, pltpu.roll(ext, text - 1, 0)[mid], 0.0)
        conv = prev * cw_ref[0:1, cols] + ext[mid] * cw_ref[1:2, cols] + nxt * cw_ref[2:3, cols] + cb_ref[:, cols]
        scale = MLSTM_QK ** -0.5 if j >= nqk // (2 * step) else 1.0
        qk_ref[:, cols] = (_silu(conv) * scale).astype(BF16)
    v_ref[...] = _dot(hext, w_ref[:, nqk:nqk + nv])[mid].astype(BF16)
    o_ref[...] = _dot(hext, w_ref[:, nqk + nv:nqk + 2 * nv])[mid]
    g_ref[...] = _dot(hext, w_ref[:, nqk + 2 * nv:])[mid] + gb_ref[...]


def _odd_in(X, sh, sc, g1, w, gate_b, cw, cb, *, ctx_len):
    nb, t, d = X.shape
    tm = TM_ODD
    nqk = 2 * MLSTM_HEADS * MLSTM_QK
    nv = MLSTM_HEADS * MLSTM_V
    row = pl.BlockSpec((None, 1, d), lambda b, i: (b, 0, 0))
    full = lambda a: pl.BlockSpec(a.shape, lambda b, i: (0,) * a.ndim)
    tile = lambda wd: pl.BlockSpec((None, tm, wd), lambda b, i: (b, i, 0))
    x_specs = [pl.BlockSpec((None, rows, d), lambda b, i, blk=blk: (b, blk(i), 0))
               for rows, blk in _halo_blocks(t, tm, SUBLANES)]
    return pl.pallas_call(
        functools.partial(_odd_in_kernel, seq_len=t, ctx_len=ctx_len, n_lat_batches=nb - 1),
        out_shape=[jax.ShapeDtypeStruct((nb, t, nqk), BF16), jax.ShapeDtypeStruct((nb, t, nv), BF16),
                   jax.ShapeDtypeStruct((nb, t, nv), F32), jax.ShapeDtypeStruct((nb, t, LANES), F32)],
        grid=(nb, t // tm),
        in_specs=x_specs + [row, row, full(g1), full(w), full(gate_b), full(cw), full(cb)],
        out_specs=[tile(nqk), tile(nv), tile(nv), tile(LANES)],
        compiler_params=_cparams(("parallel", "parallel")),
        name="odd_in",
    )(X, X, X, sh, sc, g1, w, gate_b, cw, cb)


def _log_sigmoid(x):
    return jnp.minimum(x, 0.0) - jnp.log(1.0 + jnp.exp(-jnp.abs(x)))


def _running_max(x, rev):
    n = x.shape[0]
    row = lax.broadcasted_iota(jnp.int32, x.shape, 0)
    k = 1
    while k < n:
        if rev:
            x = jnp.where(row < n - k, jnp.maximum(x, pltpu.roll(x, n - k, 0)), x)
        else:
            x = jnp.where(row >= k, jnp.maximum(x, pltpu.roll(x, k, 0)), x)
        k *= 2
    return x


def _mlstm_kernel(qf_ref, kf_ref, vf_ref, gf_ref, qb_ref, kb_ref, vb_ref, gb_ref, hf_ref, hb_ref, cn_scr, m_scr):
    @pl.when(pl.program_id(1) == 0)
    def _():
        cn_scr[...] = jnp.zeros_like(cn_scr)
        m_scr[...] = jnp.zeros_like(m_scr)

    dirs = (0, 1)
    q_refs, k_refs, v_refs = (qf_ref, qb_ref), (kf_ref, kb_ref), (vf_ref, vb_ref)
    g_refs, h_refs = (gf_ref, gb_ref), (hf_ref, hb_ref)
    L = qf_ref.shape[0]
    nh = MLSTM_HEADS
    i_off = (0, 2 * nh)
    last = (L - 1, 0)
    r_i = lax.broadcasted_iota(jnp.int32, (L, L), 0)
    c_i = lax.broadcasted_iota(jnp.int32, (L, L), 1)
    tri = (c_i <= r_i, c_i >= r_i)
    g = [g_refs[d][...] for d in dirs]
    b_f = [jnp.dot(tri[d].astype(F32), _log_sigmoid(g[d]), precision=lax.Precision.HIGHEST,
                   preferred_element_type=F32) for d in dirs]
    b = [pltpu.roll(b_f[d], LANES - nh, 1) for d in dirs]
    u = [g[d] - b[d] for d in dirs]
    m_prev = [m_scr[d] for d in dirs]
    cmax = [_running_max(u[d], d == 1) for d in dirs]
    mm = [jnp.maximum(m_prev[d], cmax[d]) for d in dirs]
    em = [jnp.exp(-(b[d] + mm[d])) for d in dirs]
    b_last = [b[d][last[d]:last[d] + 1, :] for d in dirs]
    a = [b_last[d] - b[d] + g[d] for d in dirs]
    m_new = [jnp.maximum(b_last[d] + m_prev[d], jnp.max(a[d], axis=0, keepdims=True)) for d in dirs]
    decay = [jnp.exp(b_last[d] + m_prev[d] - m_new[d]) for d in dirs]
    w = [jnp.exp(a[d] - m_new[d]) for d in dirs]
    for d in dirs:
        m_scr[d] = m_new[d]
    u_t = [u[d].T for d in dirs]
    w_t = [w[d].T for d in dirs]
    ones = jnp.ones((L, LANES), BF16)
    row = lax.broadcasted_iota(jnp.int32, (LANES, L), 0)
    for pair in range(nh // 2):
        pc = slice(pair * LANES, (pair + 1) * LANES)
        qp = [q_refs[d][:, pc] for d in dirs]
        kt = [k_refs[d][:, pc].astype(F32).T for d in dirs]
        ch = [(d, 2 * pair + half, i_off[d] + 2 * pair + half, half) for half in range(2) for d in dirs]
        hc = lambda hd: slice(hd * LANES, (hd + 1) * LANES)
        kt_h = [jnp.where((row < MLSTM_QK) if half == 0 else (row >= MLSTM_QK), kt[d], 0.0)
                for d, hd, col, half in ch]
        v1 = [jnp.concatenate([v_refs[d][:, hc(hd)], ones], axis=1) for d, hd, col, half in ch]
        mm_b = [jnp.broadcast_to(mm[d][:, col:col + 1], (L, LANES)) for d, hd, col, half in ch]
        em_b = [jnp.broadcast_to(em[d][:, col:col + 1], (L, LANES)) for d, hd, col, half in ch]
        dw = [jnp.where(tri[d], jnp.exp(u_t[d][col:col + 1, :] - mm_b[i]), 0.0) for i, (d, hd, col, half) in enumerate(ch)]
        s = [_dot(qp[d], kt_h[i].astype(BF16)) * dw[i] for i, (d, hd, col, half) in enumerate(ch)]
        cn = [cn_scr[d, hd] for d, hd, col, half in ch]
        qc = [_dot(qp[d], cn[i].astype(BF16)) for i, (d, hd, col, half) in enumerate(ch)]
        sv = [_dot(s[i].astype(BF16), v1[i]) for i in range(len(ch))]
        w_inter = [jnp.exp(m_prev[d][:, col:col + 1] - mm_b[i]) for i, (d, hd, col, half) in enumerate(ch)]
        num = [w_inter[i] * qc[i][:, :LANES] + sv[i][:, :LANES] for i in range(len(ch))]
        den = [w_inter[i] * qc[i][:, LANES:] + sv[i][:, LANES:] for i in range(len(ch))]
        for i, (d, hd, col, half) in enumerate(ch):
            h_refs[d][:, hc(hd)] = num[i] * (1.0 / jnp.maximum(jnp.abs(den[i]), em_b[i]))
        ktw = [(kt_h[i] * w_t[d][col:col + 1, :]).astype(BF16) for i, (d, hd, col, half) in enumerate(ch)]
        upd = [_dot(ktw[i], v1[i]) for i in range(len(ch))]
        for i, (d, hd, col, half) in enumerate(ch):
            cn_scr[d, hd] = decay[d][:, col:col + 1] * cn[i] + upd[i]


def _mlstm(QK, V, G, *, ctx_len):
    nb, t, nqk = QK.shape
    nbl = nb - 1
    L = ML_CHUNK
    ncc, nlc = ctx_len // L, t // L
    nq = nqk // 2
    nv = V.shape[-1]

    def spec(width, col, rev):
        def index(b, c):
            cc, lc = (ncc - 1 - c, nlc - 1 - (c - ncc)) if rev else (c, c - ncc)
            is_ctx = c < ncc
            return jnp.where(is_ctx, nbl, b), jnp.where(is_ctx, b * ncc + cc, lc), col
        return pl.BlockSpec((None, L, width), index)

    ins = lambda rev: [spec(nq, 0, rev), spec(nq, 1, rev), spec(nv, 0, rev), spec(LANES, 0, rev)]
    out = jax.ShapeDtypeStruct((nb, t, nv), F32)
    return pl.pallas_call(
        _mlstm_kernel,
        out_shape=[out, out],
        grid=(nbl, ncc + nlc),
        in_specs=ins(False) + ins(True),
        out_specs=[spec(nv, 0, False), spec(nv, 0, True)],
        scratch_shapes=[pltpu.VMEM((2, MLSTM_HEADS, LANES, MLSTM_V + LANES), F32),
                        pltpu.VMEM((2, 1, LANES), F32)],
        compiler_params=_cparams(("parallel", "arbitrary")),
        name="mlstm",
    )(QK, QK, V, G, QK, QK, V, G)


def kernel(x, c, ctx, c_ctx, ada_w, ada_b, norm1_g, norm2_g, ffn_w_up, ffn_conv_w, ffn_conv_b, ffn_w_down,
           att_w_in, mla_qa_g, mla_w_qb, mla_kva_g, mla_w_kvb, mla_q_g, mla_k_g, gqa_q_g, gqa_k_g, att_w_out,
           ml_w_in, ml_conv_w, ml_conv_b, ml_gate_b, ml_out_g, ml_w_out):
    B, T, D = x.shape
    ctx_len = ctx.shape[1]
    depth = ada_w.shape[0]
    assert B * ctx_len == T, "context sequences must tile one latent-length row of the token array"
    NB = B + 1

    tok = _Tokens(x, ctx.reshape(1, T, D))

    rows = -(-NB // SUBLANES) * SUBLANES
    cvec = jnp.zeros((rows, D), F32).at[:B].set(c).at[B].set(c_ctx)
    mod = _adaln(cvec, ada_w, ada_b)
    mod = mod.reshape(depth, rows, 6, 1, D).transpose(0, 2, 1, 3, 4)

    rope_m = _rope_tables(T, MLA_ROPE, MLA_NOPE)
    rope_g = _rope_tables(T, GQA_DIM, 0)

    for layer in range(depth):
        last = layer == depth - 1
        j = layer // 2
        sh1, sc1, gt1, sh2, sc2, gt2 = (mod[layer, i] for i in range(6))
        g1 = norm1_g[layer][None, :]
        g2 = norm2_g[layer][None, :]
        if layer % 2 == 0:
            ew = _even_weights(att_w_in[j], mla_w_qb[j], mla_w_kvb[j], mla_q_g[j], mla_k_g[j],
                               gqa_q_g[j], gqa_k_g[j])
            qm, km, vm, qg, kg, vg = _even_in(tok, sh1, sc1, g1, ew, mla_qa_g[j][None, :], mla_kva_g[j][None, :],
                                              rope_m, rope_g)
            oa = _attention(qm, km, vm, k_share=1, v_share=2, ctx_len=ctx_len)
            ob = _attention(qg, kg, vg, k_share=GQA_HEADS // GQA_KV_HEADS, v_share=GQA_HEADS // GQA_KV_HEADS,
                            ctx_len=ctx_len)
            mixer, mix_in, out_g, w_out = "attn", (oa, ob), None, att_w_out[j]
        else:
            assert tok.ctx is None, "an odd layer reads the assembled token array"
            gate_b = _pad_cols(ml_gate_b[j][None, :], LANES)
            w_in = jnp.pad(ml_w_in[j], ((0, 0), (0, LANES - 4 * MLSTM_HEADS))).astype(BF16)
            qk, v, o, gates = _odd_in(tok.lat, sh1, sc1, g1, w_in, gate_b, ml_conv_w[j], ml_conv_b[j][None, :],
                                      ctx_len=ctx_len)
            hf, hb = _mlstm(qk, v, gates, ctx_len=ctx_len)
            mixer, mix_in, out_g, w_out = "mlstm", (hf, hb, o), ml_out_g[j].reshape(1, -1), ml_w_out[j]
        X = _mix_ffn(mixer, tok, mix_in, out_g, w_out.astype(BF16), gt1, sh2, sc2, gt2, g2,
                     ffn_w_up[layer].astype(BF16), ffn_conv_w[layer], ffn_conv_b[layer][None, :],
                     ffn_w_down[layer].astype(BF16),
                     n_out_batches=B if last else NB, n_lat_batches=B, ctx_len=ctx_len)
        tok = _Tokens(X)
    return X[:B]
```

```python
import functools

import numpy as np
import jax
import jax.numpy as jnp
from jax import lax
from jax.experimental import pallas as pl
from jax.experimental.pallas import tpu as pltpu

F32 = jnp.float32
BF16 = jnp.bfloat16

EPS = 1e-6
ROPE_THETA = 10000.0
GRID_W = 64
LOG2E = 1.4426950408889634

MLA_HEADS = 8
MLA_Q_RANK = 384
MLA_KV_RANK = 256
MLA_NOPE = 64
MLA_ROPE = 32
MLA_V = 64
MLA_QK = MLA_NOPE + MLA_ROPE
GQA_HEADS = 8
GQA_KV_HEADS = 2
GQA_DIM = 64
MLSTM_HEADS = 8
MLSTM_QK = 64
MLSTM_V = 128
D_FF = 2816

LANES = 128
SUBLANES = 8
VMEM_LIMIT = 56 * 1024 * 1024

TM_IN = 256
TM_ODD = 512
TM_FFN = 512
FF_CHUNK = 256
TQ = 512
KV_CHUNK = 512
ML_CHUNK = 128


def _cparams(sem):
    return pltpu.CompilerParams(dimension_semantics=sem, vmem_limit_bytes=VMEM_LIMIT)


def _dot(a, b):
    return jnp.dot(a, b, preferred_element_type=F32)


def _dot_nt(a, b):
    return lax.dot_general(a, b, (((1,), (1,)), ((), ())), preferred_element_type=F32)


def _rms(x, gain_row, inv_n):
    ss = jnp.sum(x * x, axis=-1, keepdims=True) * inv_n
    return x * lax.rsqrt(ss + EPS) * gain_row


def _norm_mod(x, g_row, sh_row, sc_row):
    ms = jnp.mean(x * x, axis=-1, keepdims=True)
    return x * lax.rsqrt(ms + EPS) * (g_row * (1.0 + sc_row)) + sh_row


def _norm_mod_parts(x, g_row, sh_row, sc_row, n_parts=3):
    rp = x.shape[0] // n_parts
    assert rp * n_parts == x.shape[0] and rp % (2 * SUBLANES) == 0
    return [_norm_mod(x[k * rp:(k + 1) * rp], g_row, sh_row, sc_row).astype(BF16) for k in range(n_parts)]


def _rdot(lhs, w):
    if isinstance(lhs, (list, tuple)):
        return jnp.concatenate([_dot(p, w) for p in lhs], axis=0)
    return _dot(lhs, w)


def _silu(x):
    return x * (1.0 / (1.0 + jnp.exp(-x)))


def _edge_masks(row0, n_rows, is_latent, seq_len, ctx_len):
    assert seq_len & (seq_len - 1) == 0 and ctx_len & (ctx_len - 1) == 0, "sequence lengths must be powers of two"
    last = jnp.where(is_latent, seq_len - 1, ctx_len - 1)
    pos = (row0 + lax.broadcasted_iota(jnp.int32, (n_rows, 1), 0)) & last
    return pos != 0, pos != last


def _lockstep(chains):
    chains = list(chains)
    while chains:
        for c in list(chains):
            try:
                next(c)
            except StopIteration:
                chains.remove(c)


class _Tokens:
    def __init__(self, lat, ctx=None):
        self.lat, self.ctx = lat, ctx
        self.nb = lat.shape[0] + (0 if ctx is None else 1)
        self.t, self.d = lat.shape[1], lat.shape[2]
        self.n_refs = 1 if ctx is None else 2

    def specs(self, rows, blk):
        if self.ctx is None:
            return [self.lat], [pl.BlockSpec((None, rows, self.d), lambda b, i: (b, blk(i), 0))]
        nbl = self.nb - 1
        lat = pl.BlockSpec((None, rows, self.d), lambda b, i: (jnp.minimum(b, nbl - 1), jnp.where(b < nbl, blk(i), 0), 0))
        ctx = pl.BlockSpec((None, rows, self.d), lambda b, i: (0, jnp.where(b < nbl, 0, blk(i)), 0))
        return [self.lat, self.ctx], [lat, ctx]


def _tok(refs, is_latent):
    if len(refs) == 1:
        return refs[0][...]
    return jnp.where(is_latent, refs[0][...], refs[1][...])


def _adaln_kernel(c_ref, w_ref, b_ref, o_ref):
    c = _silu(c_ref[...]).astype(BF16)
    o_ref[...] = _dot(c, w_ref[...].astype(BF16)) + b_ref[...]


def _adaln(cvec, ada_w, ada_b):
    depth, d, n = ada_w.shape
    rows = cvec.shape[0]
    tn = 1536
    return pl.pallas_call(
        _adaln_kernel,
        out_shape=jax.ShapeDtypeStruct((depth, rows, n), F32),
        grid=(depth, n // tn),
        in_specs=[
            pl.BlockSpec((rows, d), lambda l, j: (0, 0)),
            pl.BlockSpec((None, d, tn), lambda l, j: (l, 0, j)),
            pl.BlockSpec((None, 1, tn), lambda l, j: (l, 0, j)),
        ],
        out_specs=pl.BlockSpec((None, rows, tn), lambda l, j: (l, 0, j)),
        compiler_params=_cparams(("arbitrary", "arbitrary")),
        name="adaln",
    )(cvec, ada_w, ada_b.reshape(depth, 1, n))


def _rope(y, cos, sa, sb, shift):
    return y * cos + pltpu.roll(y, LANES - shift, 1) * sa + pltpu.roll(y, shift, 1) * sb


def _pair_sumsq(y):
    n = y.shape[1]
    r = lax.broadcasted_iota(jnp.int32, (n, n), 0) // LANES
    c = lax.broadcasted_iota(jnp.int32, (n, n), 1) // LANES
    return _dot((y * y).astype(BF16), jnp.where(r == c, 1.0, 0.0).astype(BF16))


def _rope_pair(y, tabs, shift):
    cos, sa, sb = tabs
    return jnp.concatenate([_rope(y[:, :LANES], cos, sa, sb, shift), _rope(y[:, LANES:], cos, sa, sb, shift)], axis=1)


def _even_in_kernel(*refs, n_tok_refs, n_lat_batches):
    x_refs, refs = refs[:n_tok_refs], refs[n_tok_refs:]
    (sh_ref, sc_ref, g1_ref, win_ref, gqa_ref, wqb_ref, gkva_ref, wk_ref, wv_ref,
     gmq_ref, gmk_ref, ggq_ref, ggk_ref, rm_ref, rg_ref,
     qm_ref, km_ref, vm_ref, qg_ref, kg_ref, vg_ref) = refs
    x = _tok(x_refs, pl.program_id(0) < n_lat_batches)
    hparts = _norm_mod_parts(x, g1_ref[...], sh_ref[...], sc_ref[...], 2)
    h = jnp.concatenate(hparts, axis=0)
    rope_m = (rm_ref[0], rm_ref[1], rm_ref[2])
    rope_g = (rg_ref[0], rg_ref[1], rg_ref[2])
    pw = 2 * LANES
    o_ckv = MLA_Q_RANK
    o_kr = o_ckv + MLA_KV_RANK
    o_gq = o_kr + LANES
    o_gk = o_gq + GQA_HEADS * LANES
    o_gv = o_gk + GQA_KV_HEADS * LANES
    two = lambda a: jnp.concatenate([a, a], axis=1)

    low = _rdot(hparts, win_ref[:, 0:o_gq])
    cqn = _rms(low[:, 0:o_ckv], gqa_ref[...], 1.0 / MLA_Q_RANK).astype(BF16)
    ckvn = _rms(low[:, o_ckv:o_kr], gkva_ref[...], 1.0 / MLA_KV_RANK).astype(BF16)
    krp = low[:, o_kr:o_gq]
    gmk = gmk_ref[...]
    kr_rot = two(_rope(krp * gmk[:, :LANES], *rope_m, MLA_ROPE // 4))
    kr_ss = _pair_sumsq(two(krp))
    vm_ref[...] = _dot(ckvn, wv_ref[...]).astype(BF16)
    vg_ref[...] = _dot(h, win_ref[:, o_gv:o_gv + GQA_KV_HEADS * LANES]).astype(BF16)

    def normed_roped(lhs, w_ref, c0, gain_ref, inv_n, tabs, shift, out_ref, p):
        blk = _dot(lhs, w_ref[:, c0 + p * pw:c0 + (p + 1) * pw])
        yield
        ss = _pair_sumsq(blk)
        yield
        y = blk * lax.rsqrt(ss * inv_n + EPS) * gain_ref[...]
        yield
        out_ref[:, p * pw:(p + 1) * pw] = _rope_pair(y, tabs, shift).astype(BF16)

    def mla_key(p):
        blk = _dot(ckvn, wk_ref[:, p * pw:(p + 1) * pw])
        yield
        ss = _pair_sumsq(blk)
        yield
        r = lax.rsqrt((ss + kr_ss) * (1.0 / MLA_QK) + EPS)
        yield
        km_ref[:, p * pw:(p + 1) * pw] = (r * (blk * gmk + kr_rot)).astype(BF16)

    for p in range(MLA_HEADS // 2):
        chains = [normed_roped(cqn, wqb_ref, 0, gmq_ref, 1.0 / MLA_QK, rope_m, MLA_ROPE // 4, qm_ref, p),
                  mla_key(p),
                  normed_roped(h, win_ref, o_gq, ggq_ref, 1.0 / GQA_DIM, rope_g, GQA_DIM // 4, qg_ref, p)]
        if p < GQA_KV_HEADS // 2:
            chains.append(normed_roped(h, win_ref, o_gk, ggk_ref, 1.0 / GQA_DIM, rope_g, GQA_DIM // 4, kg_ref, p))
        _lockstep(chains)


def _pad_cols(w, width):
    return jnp.pad(w, ((0, 0), (0, width - w.shape[1])))


def _even_weights(w_in, w_qb, w_kvb, g_mq, g_mk, g_gq, g_gk):
    d = w_in.shape[0]
    s0 = MLA_Q_RANK
    s1 = s0 + MLA_KV_RANK
    s2 = s1 + MLA_ROPE
    s3 = s2 + GQA_HEADS * GQA_DIM
    s4 = s3 + GQA_KV_HEADS * GQA_DIM
    cq, ckv, kr = w_in[:, :s0], w_in[:, s0:s1], w_in[:, s1:s2]
    gq, gk, gv = w_in[:, s2:s3], w_in[:, s3:s4], w_in[:, s4:]
    kr_p = jnp.pad(kr, ((0, 0), (MLA_NOPE, LANES - MLA_QK)))
    gq_p = jnp.pad(gq.reshape(d, GQA_HEADS, GQA_DIM), ((0, 0), (0, 0), (0, LANES - GQA_DIM))).reshape(d, -1)
    gk_p = jnp.pad(gk.reshape(d, GQA_KV_HEADS, GQA_DIM), ((0, 0), (0, 0), (0, LANES - GQA_DIM))).reshape(d, -1)
    gv3 = gv.reshape(d, GQA_KV_HEADS, GQA_DIM)
    gv_p = jnp.concatenate([gv3, gv3], axis=-1).reshape(d, -1)
    win = jnp.concatenate([cq, ckv, kr_p, gq_p, gk_p, gv_p], axis=1).astype(BF16)
    wqb = jnp.pad(w_qb.reshape(MLA_Q_RANK, MLA_HEADS, MLA_QK),
                  ((0, 0), (0, 0), (0, LANES - MLA_QK))).reshape(MLA_Q_RANK, -1).astype(BF16)
    kv3 = w_kvb.reshape(MLA_KV_RANK, MLA_HEADS, MLA_NOPE + MLA_V)
    wk = jnp.pad(kv3[:, :, :MLA_NOPE], ((0, 0), (0, 0), (0, LANES - MLA_NOPE))).reshape(MLA_KV_RANK, -1).astype(BF16)
    wv = kv3[:, :, MLA_NOPE:].reshape(MLA_KV_RANK, -1).astype(BF16)
    pair = lambda g: jnp.tile(_pad_cols(g[None, :], LANES), (1, 2))
    gmq = pair(g_mq * (MLA_QK ** -0.5 * LOG2E))
    gmk = pair(g_mk)
    ggq = pair(g_gq * (GQA_DIM ** -0.5 * LOG2E))
    ggk = pair(g_gk)
    return win, wqb, wk, wv, gmq, gmk, ggq, ggk


def _rope_tables(n_tokens, rot_dim, offset):
    rows = n_tokens // GRID_W
    row = np.repeat(np.arange(rows), GRID_W).astype(np.float32)
    col = np.tile(np.arange(GRID_W), rows).astype(np.float32)
    n_freq = rot_dim // 4
    inv = (np.float32(ROPE_THETA) ** (-np.arange(n_freq, dtype=np.float32) / np.float32(n_freq))).astype(np.float32)
    a_r, a_c = row[:, None] * inv, col[:, None] * inv
    ang = np.concatenate([a_r, a_r, a_c, a_c], axis=-1)
    cos, sin = np.cos(ang), np.sin(ang)
    quarter = (np.arange(rot_dim) // n_freq) % 2
    sa = np.where(quarter == 0, -sin, 0.0)
    sb = np.where(quarter == 1, sin, 0.0)
    pad = ((0, 0), (offset, LANES - offset - rot_dim))
    lat = np.stack([np.pad(cos, pad, constant_values=1.0), np.pad(sa, pad), np.pad(sb, pad)])
    ident = np.stack([np.ones((n_tokens, LANES)), np.zeros((n_tokens, LANES)), np.zeros((n_tokens, LANES))])
    return jnp.asarray(np.stack([lat, ident]), F32)


def _even_in(tok, sh, sc, g1, ew, gqa, gkva, rope_m, rope_g):
    nb, t, d = tok.nb, tok.t, tok.d
    win, wqb, wk, wv, gmq, gmk, ggq, ggk = ew
    tm = TM_IN
    full = lambda a: pl.BlockSpec(a.shape, lambda b, i: (0,) * a.ndim)
    row = pl.BlockSpec((None, 1, d), lambda b, i: (b, 0, 0))
    rope_spec = pl.BlockSpec((None, 3, tm, LANES), lambda b, i: (jnp.where(b == nb - 1, 1, 0), 0, i, 0))
    x_args, x_specs = tok.specs(tm, lambda i: i)

    def out(width):
        return jax.ShapeDtypeStruct((nb, t, width), BF16), pl.BlockSpec((None, tm, width), lambda b, i: (b, i, 0))

    outs = [out(MLA_HEADS * LANES), out(MLA_HEADS * LANES), out(MLA_HEADS * MLA_V),
            out(GQA_HEADS * LANES), out(GQA_KV_HEADS * LANES), out(GQA_KV_HEADS * LANES)]
    return pl.pallas_call(
        functools.partial(_even_in_kernel, n_tok_refs=tok.n_refs, n_lat_batches=nb - 1),
        out_shape=[o[0] for o in outs],
        grid=(nb, t // tm),
        in_specs=x_specs + [row, row, full(g1), full(win), full(gqa),
                            full(wqb), full(gkva), full(wk), full(wv), full(gmq), full(gmk), full(ggq), full(ggk),
                            rope_spec, rope_spec],
        out_specs=[o[1] for o in outs],
        compiler_params=_cparams(("parallel", "parallel")),
        name="even_in",
    )(*x_args, sh, sc, g1, win, gqa, wqb, gkva, wk, wv, gmq, gmk, ggq, ggk, rope_m, rope_g)


def _attn_kernel(q_ref, kl_ref, kc_ref, vl_ref, vc_ref, o_ref, s_scr, p_scr, *,
                 n_heads, k_share, v_share, ctx_len, n_lat_batches):
    tq = q_ref.shape[0]

    def run(q0, nq, parts):
        lane = lax.broadcasted_iota(jnp.int32, (nq, LANES), 1)
        chunks, off = [], 0
        for k_ref, _, base, n in parts:
            size = min(KV_CHUNK, n)
            for c in range(n // size):
                chunks.append((k_ref, base + c * size, size, off + c * size))
            off += n
        mx, m, lsum = {}, {}, {}
        outs = [None] * n_heads

        def scores(hd, ci):
            k_ref, r0, size, c0 = chunks[ci]
            kb = hd // k_share
            s = _dot_nt(q_ref[q0:q0 + nq, hd * LANES:(hd + 1) * LANES],
                        k_ref[pl.ds(r0, size), kb * LANES:(kb + 1) * LANES])
            s_scr[hd % 2, 0:nq, c0:c0 + size] = s
            for j in range(size // LANES):
                blk = s[:, j * LANES:(j + 1) * LANES]
                mx[hd] = blk if hd not in mx else jnp.maximum(mx[hd], blk)

        def values(hd, ci):
            _, _, size, c0 = chunks[ci]
            p = jnp.exp2(s_scr[hd % 2, 0:nq, c0:c0 + size] - m[hd])
            for j in range(size // LANES):
                blk = p[:, j * LANES:(j + 1) * LANES]
                lsum[hd] = blk if hd not in lsum else lsum[hd] + blk
            p_scr[hd % 2, 0:nq, c0:c0 + size] = p.astype(BF16)

        def product(hd):
            vb = hd // v_share
            out, off = None, 0
            for _, v_ref, base, n in parts:
                part = _dot(p_scr[hd % 2, 0:nq, off:off + n], v_ref[pl.ds(base, n), vb * LANES:(vb + 1) * LANES])
                out = part if out is None else out + part
                off += n
            return out

        for j in range(n_heads + 1):
            for ci in range(len(chunks)):
                if j < n_heads:
                    scores(j, ci)
                if j >= 1:
                    values(j - 1, ci)
            if j < n_heads:
                m[j] = jnp.max(mx.pop(j), axis=-1, keepdims=True)
            if j >= 1:
                outs[j - 1] = product(j - 1) * (1.0 / jnp.sum(lsum.pop(j - 1), axis=-1, keepdims=True))
        for j in range(n_heads // 2):
            pair = jnp.where(lane < LANES // 2, outs[2 * j], outs[2 * j + 1])
            o_ref[q0:q0 + nq, j * LANES:(j + 1) * LANES] = pair.astype(o_ref.dtype)

    b = pl.program_id(0)
    is_lat = b < n_lat_batches
    per_blk = kc_ref.shape[0] // ctx_len

    @pl.when(is_lat)
    def _():
        own = pl.multiple_of((b % per_blk) * ctx_len, ctx_len)
        run(0, tq, [(kc_ref, vc_ref, own, ctx_len), (kl_ref, vl_ref, 0, kl_ref.shape[0])])

    @pl.when(jnp.logical_not(is_lat))
    def _():
        for sub in range(tq // ctx_len):
            run(sub * ctx_len, ctx_len, [(kc_ref, vc_ref, sub * ctx_len, ctx_len)])


def _attention(Q, K, V, *, k_share, v_share, ctx_len):
    nb, t, qw = Q.shape
    nbl = nb - 1
    n_heads = qw // LANES
    assert TQ % ctx_len == 0 and nbl * ctx_len == t
    per_blk = TQ // ctx_len
    kw, vw, ow = K.shape[-1], V.shape[-1], qw // 2
    lat_b = lambda b: jnp.minimum(b, nbl - 1)
    ctx_blk = lambda b, i: jnp.where(b < nbl, b // per_blk, i)
    return pl.pallas_call(
        functools.partial(_attn_kernel, n_heads=n_heads, k_share=k_share, v_share=v_share, ctx_len=ctx_len,
                          n_lat_batches=nbl),
        out_shape=jax.ShapeDtypeStruct((nb, t, ow), BF16),
        grid=(nb, t // TQ),
        in_specs=[
            pl.BlockSpec((None, TQ, qw), lambda b, i: (b, i, 0)),
            pl.BlockSpec((None, t, kw), lambda b, i: (lat_b(b), 0, 0)),
            pl.BlockSpec((None, TQ, kw), lambda b, i: (nbl, ctx_blk(b, i), 0)),
            pl.BlockSpec((None, t, vw), lambda b, i: (lat_b(b), 0, 0)),
            pl.BlockSpec((None, TQ, vw), lambda b, i: (nbl, ctx_blk(b, i), 0)),
        ],
        out_specs=pl.BlockSpec((None, TQ, ow), lambda b, i: (b, i, 0)),
        scratch_shapes=[pltpu.VMEM((2, TQ, t + ctx_len), F32), pltpu.VMEM((2, TQ, t + ctx_len), BF16)],
        compiler_params=_cparams(("parallel", "arbitrary")),
        name="attn_k%d_v%d" % (k_share, v_share),
    )(Q, K, K, V, V)


def _ext_rows(main, prev, nxt):
    hp = prev.shape[0]
    return jnp.concatenate([prev.astype(F32)[hp - SUBLANES:hp], main.astype(F32), nxt.astype(F32)[0:SUBLANES]], axis=0)


def _mix_ffn_kernel(*refs, mixer, n_tok_refs, seq_len, ctx_len, n_lat_batches):
    is_latent = pl.program_id(0) < n_lat_batches
    x_parts = [_tok(refs[k * n_tok_refs:(k + 1) * n_tok_refs], is_latent) for k in range(3)]
    refs = refs[3 * n_tok_refs:]
    ext = lambda trio: _ext_rows(*(r[...] for r in trio))
    if mixer == "attn":
        oa_refs, ob_refs, refs = refs[:3], refs[3:6], refs[6:]
    else:
        hf_refs, hb_refs, o_refs, og_ref, refs = refs[:3], refs[3:6], refs[6:9], refs[9], refs[10:]
    (wo_ref, gt1_ref, sh_ref, sc_ref, gt2_ref, g2_ref, wu_ref, cw_ref, cb_ref, wd_ref, y_ref, a_scr) = refs[:12]
    tm = y_ref.shape[0]
    text = tm + 2 * SUBLANES
    mid = slice(SUBLANES, SUBLANES + tm)

    n_parts = 3
    rp = text // n_parts
    assert rp * n_parts == text and rp % (2 * SUBLANES) == 0
    x_raw = _ext_rows(*x_parts)
    if mixer == "attn":
        oa = ext(oa_refs).astype(BF16)
        ob = ext(ob_refs).astype(BF16)
        na = oa.shape[1]
    else:
        m_scr = refs[12]
        hf, hb, og = ext(hf_refs), ext(hb_refs), ext(o_refs)
    xparts, hparts = [], []
    for k in range(n_parts):
        rows = slice(k * rp, (k + 1) * rp)
        if mixer == "attn":
            mix = _dot(oa[rows], wo_ref[0:na, :]) + _dot(ob[rows], wo_ref[na:, :])
        else:
            for hd in range(MLSTM_HEADS):
                cols = slice(hd * LANES, (hd + 1) * LANES)
                hn = _rms(hf[rows, cols] + hb[rows, cols], og_ref[:, cols], 1.0 / MLSTM_V)
                m_scr[rows, cols] = (hn * (1.0 / (1.0 + jnp.exp(-og[rows, cols])))).astype(BF16)
            mix = _dot(m_scr[rows, :], wo_ref[...])
        xk = x_raw[rows] + gt1_ref[...] * mix
        xparts.append(xk)
        hparts.append(_norm_mod(xk, g2_ref[...], sh_ref[...], sc_ref[...]).astype(BF16))
    xext = jnp.concatenate(xparts, axis=0)
    hext = jnp.concatenate(hparts, axis=0)
    has_prev, has_next = _edge_masks(pl.program_id(1) * tm, tm, is_latent, seq_len, ctx_len)
    for j in range(D_FF // FF_CHUNK):
        cols = slice(j * FF_CHUNK, (j + 1) * FF_CHUNK)
        lhs = hparts if j == 0 else hext
        gext = _rdot(lhs, wu_ref[:, cols])
        val = _rdot(lhs, wu_ref[:, D_FF + j * FF_CHUNK:D_FF + (j + 1) * FF_CHUNK])[mid]
        prev = jnp.where(has_prev, pltpu.roll(gext, 1, 0)[mid], 0.0)
        nxt = jnp.where(has_next, pltpu.roll(gext, text - 1, 0)[mid], 0.0)
        conv = prev * cw_ref[0:1, cols] + gext[mid] * cw_ref[1:2, cols] + nxt * cw_ref[2:3, cols] + cb_ref[:, cols]
        a_scr[:, cols] = (_silu(conv) * val).astype(BF16)
    y_ref[...] = xext[mid] + gt2_ref[...] * _dot(a_scr[...], wd_ref[...])


def _halo_blocks(t, tm, halo):
    per, nblk = tm // halo, t // halo
    return [(tm, lambda i: i), (halo, lambda i: jnp.maximum(i * per - 1, 0)),
            (halo, lambda i: jnp.minimum((i + 1) * per, nblk - 1))]


def _mix_ffn(mixer, tok, mix_in, out_g, w_out, gt1, sh, sc, gt2, g2, w_up, cw, cb, wd, *,
             n_out_batches, n_lat_batches, ctx_len):
    t, d = tok.t, tok.d
    tm = TM_FFN
    row = pl.BlockSpec((None, 1, d), lambda b, i: (b, 0, 0))
    full = lambda a: pl.BlockSpec(a.shape, lambda b, i: (0,) * a.ndim, pipeline_mode=pl.Buffered(1))
    args, specs = [], []
    for rows, blk in _halo_blocks(t, tm, SUBLANES):
        a_, s_ = tok.specs(rows, blk)
        args += a_
        specs += s_
    for a in mix_in:
        halo = SUBLANES * (4 // a.dtype.itemsize)
        for rows, blk in _halo_blocks(t, tm, halo):
            args.append(a)
            specs.append(pl.BlockSpec((None, rows, a.shape[-1]), lambda b, i, blk=blk: (b, blk(i), 0)))
    if mixer == "mlstm":
        args.append(out_g)
        specs.append(full(out_g))
    consts = [w_out, gt1, sh, sc, gt2, g2, w_up, cw, cb, wd]
    args += consts
    specs += [full(w_out), row, row, row, row] + [full(a) for a in consts[5:]]
    return pl.pallas_call(
        functools.partial(_mix_ffn_kernel, mixer=mixer, n_tok_refs=tok.n_refs, seq_len=t, ctx_len=ctx_len,
                          n_lat_batches=n_lat_batches),
        out_shape=jax.ShapeDtypeStruct((n_out_batches, t, d), F32),
        grid=(n_out_batches, t // tm),
        in_specs=specs,
        out_specs=pl.BlockSpec((None, tm, d), lambda b, i: (b, i, 0)),
        scratch_shapes=[pltpu.VMEM((tm, D_FF), BF16)]
        + ([pltpu.VMEM((tm + 2 * SUBLANES, d), BF16)] if mixer == "mlstm" else []),
        compiler_params=_cparams(("parallel", "parallel")),
        name=mixer + "_ffn",
    )(*args)


def _odd_in_kernel(x_ref, xp_ref, xn_ref, sh_ref, sc_ref, g1_ref, w_ref, gb_ref, cw_ref, cb_ref,
                   qk_ref, v_ref, o_ref, g_ref, *, seq_len, ctx_len, n_lat_batches):
    tm = x_ref.shape[0]
    text = tm + 2 * SUBLANES
    mid = slice(SUBLANES, SUBLANES + tm)
    xext = jnp.concatenate([xp_ref[...], x_ref[...], xn_ref[...]], axis=0)
    hparts = _norm_mod_parts(xext, g1_ref[...], sh_ref[...], sc_ref[...])
    hext = jnp.concatenate(hparts, axis=0)
    nqk = qk_ref.shape[1]
    nv = v_ref.shape[1]
    has_prev, has_next = _edge_masks(pl.program_id(1) * tm, tm, pl.program_id(0) < n_lat_batches, seq_len, ctx_len)
    step = 2 * LANES
    for j in range(nqk // step):
        cols = slice(j * step, (j + 1) * step)
        ext = _rdot(hparts if j == 0 else hext, w_ref[:, cols])
        prev = jnp.where(has_prev, pltpu.roll(ext, 1, 0)[mid], 0.0)
        nxt = jnp.where(has_next, pltpu.roll(ext, text - 1, 0)[mid], 0.0)
        conv = prev * cw_ref[0:1, cols] + ext[mid] * cw_ref[1:2, cols] + nxt * cw_ref[2:3, cols] + cb_ref[:, cols]
        scale = MLSTM_QK ** -0.5 if j >= nqk // (2 * step) else 1.0
        qk_ref[:, cols] = (_silu(conv) * scale).astype(BF16)
    v_ref[...] = _dot(hext, w_ref[:, nqk:nqk + nv])[mid].astype(BF16)
    o_ref[...] = _dot(hext, w_ref[:, nqk + nv:nqk + 2 * nv])[mid]
    g_ref[...] = _dot(hext, w_ref[:, nqk + 2 * nv:])[mid] + gb_ref[...]


def _odd_in(X, sh, sc, g1, w, gate_b, cw, cb, *, ctx_len):
    nb, t, d = X.shape
    tm = TM_ODD
    nqk = 2 * MLSTM_HEADS * MLSTM_QK
    nv = MLSTM_HEADS * MLSTM_V
    row = pl.BlockSpec((None, 1, d), lambda b, i: (b, 0, 0))
    full = lambda a: pl.BlockSpec(a.shape, lambda b, i: (0,) * a.ndim)
    tile = lambda wd: pl.BlockSpec((None, tm, wd), lambda b, i: (b, i, 0))
    x_specs = [pl.BlockSpec((None, rows, d), lambda b, i, blk=blk: (b, blk(i), 0))
               for rows, blk in _halo_blocks(t, tm, SUBLANES)]
    return pl.pallas_call(
        functools.partial(_odd_in_kernel, seq_len=t, ctx_len=ctx_len, n_lat_batches=nb - 1),
        out_shape=[jax.ShapeDtypeStruct((nb, t, nqk), BF16), jax.ShapeDtypeStruct((nb, t, nv), BF16),
                   jax.ShapeDtypeStruct((nb, t, nv), F32), jax.ShapeDtypeStruct((nb, t, LANES), F32)],
        grid=(nb, t // tm),
        in_specs=x_specs + [row, row, full(g1), full(w), full(gate_b), full(cw), full(cb)],
        out_specs=[tile(nqk), tile(nv), tile(nv), tile(LANES)],
        compiler_params=_cparams(("parallel", "parallel")),
        name="odd_in",
    )(X, X, X, sh, sc, g1, w, gate_b, cw, cb)


def _log_sigmoid(x):
    return jnp.minimum(x, 0.0) - jnp.log(1.0 + jnp.exp(-jnp.abs(x)))


def _running_max(x, rev):
    n = x.shape[0]
    row = lax.broadcasted_iota(jnp.int32, x.shape, 0)
    k = 1
    while k < n:
        if rev:
            x = jnp.where(row < n - k, jnp.maximum(x, pltpu.roll(x, n - k, 0)), x)
        else:
            x = jnp.where(row >= k, jnp.maximum(x, pltpu.roll(x, k, 0)), x)
        k *= 2
    return x


def _mlstm_kernel(qf_ref, kf_ref, vf_ref, gf_ref, qb_ref, kb_ref, vb_ref, gb_ref, hf_ref, hb_ref, cn_scr, m_scr):
    @pl.when(pl.program_id(1) == 0)
    def _():
        cn_scr[...] = jnp.zeros_like(cn_scr)
        m_scr[...] = jnp.zeros_like(m_scr)

    dirs = (0, 1)
    q_refs, k_refs, v_refs = (qf_ref, qb_ref), (kf_ref, kb_ref), (vf_ref, vb_ref)
    g_refs, h_refs = (gf_ref, gb_ref), (hf_ref, hb_ref)
    L = qf_ref.shape[0]
    nh = MLSTM_HEADS
    i_off = (0, 2 * nh)
    last = (L - 1, 0)
    r_i = lax.broadcasted_iota(jnp.int32, (L, L), 0)
    c_i = lax.broadcasted_iota(jnp.int32, (L, L), 1)
    tri = (c_i <= r_i, c_i >= r_i)
    g = [g_refs[d][...] for d in dirs]
    b_f = [jnp.dot(tri[d].astype(F32), _log_sigmoid(g[d]), precision=lax.Precision.HIGHEST,
                   preferred_element_type=F32) for d in dirs]
    b = [pltpu.roll(b_f[d], LANES - nh, 1) for d in dirs]
    u = [g[d] - b[d] for d in dirs]
    m_prev = [m_scr[d] for d in dirs]
    cmax = [_running_max(u[d], d == 1) for d in dirs]
    mm = [jnp.maximum(m_prev[d], cmax[d]) for d in dirs]
    em = [jnp.exp(-(b[d] + mm[d])) for d in dirs]
    b_last = [b[d][last[d]:last[d] + 1, :] for d in dirs]
    a = [b_last[d] - b[d] + g[d] for d in dirs]
    m_new = [jnp.maximum(b_last[d] + m_prev[d], jnp.max(a[d], axis=0, keepdims=True)) for d in dirs]
    decay = [jnp.exp(b_last[d] + m_prev[d] - m_new[d]) for d in dirs]
    w = [jnp.exp(a[d] - m_new[d]) for d in dirs]
    for d in dirs:
        m_scr[d] = m_new[d]
    u_t = [u[d].T for d in dirs]
    w_t = [w[d].T for d in dirs]
    ones = jnp.ones((L, LANES), BF16)
    row = lax.broadcasted_iota(jnp.int32, (LANES, L), 0)
    for pair in range(nh // 2):
        pc = slice(pair * LANES, (pair + 1) * LANES)
        qp = [q_refs[d][:, pc] for d in dirs]
        kt = [k_refs[d][:, pc].astype(F32).T for d in dirs]
        ch = [(d, 2 * pair + half, i_off[d] + 2 * pair + half, half) for half in range(2) for d in dirs]
        hc = lambda hd: slice(hd * LANES, (hd + 1) * LANES)
        kt_h = [jnp.where((row < MLSTM_QK) if half == 0 else (row >= MLSTM_QK), kt[d], 0.0)
                for d, hd, col, half in ch]
        v1 = [jnp.concatenate([v_refs[d][:, hc(hd)], ones], axis=1) for d, hd, col, half in ch]
        mm_b = [jnp.broadcast_to(mm[d][:, col:col + 1], (L, LANES)) for d, hd, col, half in ch]
        em_b = [jnp.broadcast_to(em[d][:, col:col + 1], (L, LANES)) for d, hd, col, half in ch]
        dw = [jnp.where(tri[d], jnp.exp(u_t[d][col:col + 1, :] - mm_b[i]), 0.0) for i, (d, hd, col, half) in enumerate(ch)]
        s = [_dot(qp[d], kt_h[i].astype(BF16)) * dw[i] for i, (d, hd, col, half) in enumerate(ch)]
        cn = [cn_scr[d, hd] for d, hd, col, half in ch]
        qc = [_dot(qp[d], cn[i].astype(BF16)) for i, (d, hd, col, half) in enumerate(ch)]
        sv = [_dot(s[i].astype(BF16), v1[i]) for i in range(len(ch))]
        w_inter = [jnp.exp(m_prev[d][:, col:col + 1] - mm_b[i]) for i, (d, hd, col, half) in enumerate(ch)]
        num = [w_inter[i] * qc[i][:, :LANES] + sv[i][:, :LANES] for i in range(len(ch))]
        den = [w_inter[i] * qc[i][:, LANES:] + sv[i][:, LANES:] for i in range(len(ch))]
        for i, (d, hd, col, half) in enumerate(ch):
            h_refs[d][:, hc(hd)] = num[i] * (1.0 / jnp.maximum(jnp.abs(den[i]), em_b[i]))
        ktw = [(kt_h[i] * w_t[d][col:col + 1, :]).astype(BF16) for i, (d, hd, col, half) in enumerate(ch)]
        upd = [_dot(ktw[i], v1[i]) for i in range(len(ch))]
        for i, (d, hd, col, half) in enumerate(ch):
            cn_scr[d, hd] = decay[d][:, col:col + 1] * cn[i] + upd[i]


def _mlstm(QK, V, G, *, ctx_len):
    nb, t, nqk = QK.shape
    nbl = nb - 1
    L = ML_CHUNK
    ncc, nlc = ctx_len // L, t // L
    nq = nqk // 2
    nv = V.shape[-1]

    def spec(width, col, rev):
        def index(b, c):
            cc, lc = (ncc - 1 - c, nlc - 1 - (c - ncc)) if rev else (c, c - ncc)
            is_ctx = c < ncc
            return jnp.where(is_ctx, nbl, b), jnp.where(is_ctx, b * ncc + cc, lc), col
        return pl.BlockSpec((None, L, width), index)

    ins = lambda rev: [spec(nq, 0, rev), spec(nq, 1, rev), spec(nv, 0, rev), spec(LANES, 0, rev)]
    out = jax.ShapeDtypeStruct((nb, t, nv), F32)
    return pl.pallas_call(
        _mlstm_kernel,
        out_shape=[out, out],
        grid=(nbl, ncc + nlc),
        in_specs=ins(False) + ins(True),
        out_specs=[spec(nv, 0, False), spec(nv, 0, True)],
        scratch_shapes=[pltpu.VMEM((2, MLSTM_HEADS, LANES, MLSTM_V + LANES), F32),
                        pltpu.VMEM((2, 1, LANES), F32)],
        compiler_params=_cparams(("parallel", "arbitrary")),
        name="mlstm",
    )(QK, QK, V, G, QK, QK, V, G)


def kernel(x, c, ctx, c_ctx, ada_w, ada_b, norm1_g, norm2_g, ffn_w_up, ffn_conv_w, ffn_conv_b, ffn_w_down,
           att_w_in, mla_qa_g, mla_w_qb, mla_kva_g, mla_w_kvb, mla_q_g, mla_k_g, gqa_q_g, gqa_k_g, att_w_out,
           ml_w_in, ml_conv_w, ml_conv_b, ml_gate_b, ml_out_g, ml_w_out):
    B, T, D = x.shape
    ctx_len = ctx.shape[1]
    depth = ada_w.shape[0]
    assert B * ctx_len == T, "context sequences must tile one latent-length row of the token array"
    NB = B + 1

    tok = _Tokens(x, ctx.reshape(1, T, D))

    rows = -(-NB // SUBLANES) * SUBLANES
    cvec = jnp.zeros((rows, D), F32).at[:B].set(c).at[B].set(c_ctx)
    mod = _adaln(cvec, ada_w, ada_b)
    mod = mod.reshape(depth, rows, 6, 1, D).transpose(0, 2, 1, 3, 4)

    rope_m = _rope_tables(T, MLA_ROPE, MLA_NOPE)
    rope_g = _rope_tables(T, GQA_DIM, 0)

    for layer in range(depth):
        last = layer == depth - 1
        j = layer // 2
        sh1, sc1, gt1, sh2, sc2, gt2 = (mod[layer, i] for i in range(6))
        g1 = norm1_g[layer][None, :]
        g2 = norm2_g[layer][None, :]
        if layer % 2 == 0:
            ew = _even_weights(att_w_in[j], mla_w_qb[j], mla_w_kvb[j], mla_q_g[j], mla_k_g[j],
                               gqa_q_g[j], gqa_k_g[j])
            qm, km, vm, qg, kg, vg = _even_in(tok, sh1, sc1, g1, ew, mla_qa_g[j][None, :], mla_kva_g[j][None, :],
                                              rope_m, rope_g)
            oa = _attention(qm, km, vm, k_share=1, v_share=2, ctx_len=ctx_len)
            ob = _attention(qg, kg, vg, k_share=GQA_HEADS // GQA_KV_HEADS, v_share=GQA_HEADS // GQA_KV_HEADS,
                            ctx_len=ctx_len)
            mixer, mix_in, out_g, w_out = "attn", (oa, ob), None, att_w_out[j]
        else:
            assert tok.ctx is None, "an odd layer reads the assembled token array"
            gate_b = _pad_cols(ml_gate_b[j][None, :], LANES)
            w_in = jnp.pad(ml_w_in[j], ((0, 0), (0, LANES - 4 * MLSTM_HEADS))).astype(BF16)
            qk, v, o, gates = _odd_in(tok.lat, sh1, sc1, g1, w_in, gate_b, ml_conv_w[j], ml_conv_b[j][None, :],
                                      ctx_len=ctx_len)
            hf, hb = _mlstm(qk, v, gates, ctx_len=ctx_len)
            mixer, mix_in, out_g, w_out = "mlstm", (hf, hb, o), ml_out_g[j].reshape(1, -1), ml_w_out[j]
        X = _mix_ffn(mixer, tok, mix_in, out_g, w_out.astype(BF16), gt1, sh2, sc2, gt2, g2,
                     ffn_w_up[layer].astype(BF16), ffn_conv_w[layer], ffn_conv_b[layer][None, :],
                     ffn_w_down[layer].astype(BF16),
                     n_out_batches=B if last else NB, n_lat_batches=B, ctx_len=ctx_len)
        tok = _Tokens(X)
    return X[:B]
```

```python
import functools

import numpy as np
import jax
import jax.numpy as jnp
from jax import lax
from jax.experimental import pallas as pl
from jax.experimental.pallas import tpu as pltpu

F32 = jnp.float32
BF16 = jnp.bfloat16

EPS = 1e-6
ROPE_THETA = 10000.0
GRID_W = 64
LOG2E = 1.4426950408889634

MLA_HEADS = 8
MLA_Q_RANK = 384
MLA_KV_RANK = 256
MLA_NOPE = 64
MLA_ROPE = 32
MLA_V = 64
MLA_QK = MLA_NOPE + MLA_ROPE
GQA_HEADS = 8
GQA_KV_HEADS = 2
GQA_DIM = 64
MLSTM_HEADS = 8
MLSTM_QK = 64
MLSTM_V = 128
D_FF = 2816

LANES = 128
SUBLANES = 8
VMEM_LIMIT = 56 * 1024 * 1024

TM_IN = 512
TM_ODD = 512
TM_FFN = 512
FF_CHUNK = 256
TQ = 512
KV_CHUNK = 512
ML_CHUNK = 128
ML_CHUNKS_PER_STEP = 2


def _cparams(sem):
    return pltpu.CompilerParams(dimension_semantics=sem, vmem_limit_bytes=VMEM_LIMIT)


def _dot(a, b):
    return jnp.dot(a, b, preferred_element_type=F32)


def _dot_nt(a, b):
    return lax.dot_general(a, b, (((1,), (1,)), ((), ())), preferred_element_type=F32)


def _rms(x, gain_row, inv_n):
    ss = jnp.sum(x * x, axis=-1, keepdims=True) * inv_n
    return x * lax.rsqrt(ss + EPS) * gain_row


def _norm_mod(x, g_row, sh_row, sc_row):
    ms = jnp.mean(x * x, axis=-1, keepdims=True)
    return x * lax.rsqrt(ms + EPS) * (g_row * (1.0 + sc_row)) + sh_row


def _n_row_parts(rows):
    tiles = rows // (2 * SUBLANES)
    assert tiles * 2 * SUBLANES == rows
    return next(n for n in (3, 5, 2, 1) if tiles % n == 0)


def _norm_mod_parts(x, g_row, sh_row, sc_row):
    n_parts = _n_row_parts(x.shape[0])
    rp = x.shape[0] // n_parts
    return [_norm_mod(x[k * rp:(k + 1) * rp], g_row, sh_row, sc_row).astype(BF16) for k in range(n_parts)]


def _rdot(lhs, w):
    if isinstance(lhs, (list, tuple)):
        return jnp.concatenate([_dot(p, w) for p in lhs], axis=0)
    return _dot(lhs, w)


def _silu(x):
    return x * (1.0 / (1.0 + jnp.exp(-x)))


def _edge_masks(row0, n_rows, is_latent, seq_len, ctx_len):
    assert seq_len & (seq_len - 1) == 0 and ctx_len & (ctx_len - 1) == 0, "sequence lengths must be powers of two"
    last = jnp.where(is_latent, seq_len - 1, ctx_len - 1)
    pos = (row0 + lax.broadcasted_iota(jnp.int32, (n_rows, 1), 0)) & last
    return pos != 0, pos != last


def _lockstep(chains):
    chains = list(chains)
    while chains:
        for c in list(chains):
            try:
                next(c)
            except StopIteration:
                chains.remove(c)


class _Tokens:
    def __init__(self, lat, ctx=None):
        self.lat, self.ctx = lat, ctx
        self.nb = lat.shape[0] + (0 if ctx is None else 1)
        self.t, self.d = lat.shape[1], lat.shape[2]
        self.n_refs = 1 if ctx is None else 2

    def specs(self, rows, blk):
        if self.ctx is None:
            return [self.lat], [pl.BlockSpec((None, rows, self.d), lambda b, i: (b, blk(i), 0))]
        nbl = self.nb - 1
        lat = pl.BlockSpec((None, rows, self.d), lambda b, i: (jnp.minimum(b, nbl - 1), jnp.where(b < nbl, blk(i), 0), 0))
        ctx = pl.BlockSpec((None, rows, self.d), lambda b, i: (0, jnp.where(b < nbl, 0, blk(i)), 0))
        return [self.lat, self.ctx], [lat, ctx]


def _tok(refs, is_latent):
    if len(refs) == 1:
        return refs[0][...]
    return jnp.where(is_latent, refs[0][...], refs[1][...])


def _adaln_kernel(c_ref, w_ref, b_ref, o_ref):
    c = _silu(c_ref[...]).astype(BF16)
    o_ref[...] = _dot(c, w_ref[...].astype(BF16)) + b_ref[...]


def _adaln(cvec, ada_w, ada_b):
    depth, d, n = ada_w.shape
    rows = cvec.shape[0]
    tn = 1536
    return pl.pallas_call(
        _adaln_kernel,
        out_shape=jax.ShapeDtypeStruct((depth, rows, n), F32),
        grid=(depth, n // tn),
        in_specs=[
            pl.BlockSpec((rows, d), lambda l, j: (0, 0)),
            pl.BlockSpec((None, d, tn), lambda l, j: (l, 0, j)),
            pl.BlockSpec((None, 1, tn), lambda l, j: (l, 0, j)),
        ],
        out_specs=pl.BlockSpec((None, rows, tn), lambda l, j: (l, 0, j)),
        compiler_params=_cparams(("arbitrary", "arbitrary")),
        name="adaln",
    )(cvec, ada_w, ada_b.reshape(depth, 1, n))


def _rope(y, cos, sa, sb, shift):
    return y * cos + pltpu.roll(y, LANES - shift, 1) * sa + pltpu.roll(y, shift, 1) * sb


def _pair_sumsq(y):
    n = y.shape[1]
    r = lax.broadcasted_iota(jnp.int32, (n, n), 0) // LANES
    c = lax.broadcasted_iota(jnp.int32, (n, n), 1) // LANES
    return _dot((y * y).astype(BF16), jnp.where(r == c, 1.0, 0.0).astype(BF16))


def _rope_pair(y, tabs, shift):
    cos, sa, sb = tabs
    return jnp.concatenate([_rope(y[:, :LANES], cos, sa, sb, shift), _rope(y[:, LANES:], cos, sa, sb, shift)], axis=1)


def _even_in_kernel(*refs, n_tok_refs, n_lat_batches):
    x_refs, refs = refs[:n_tok_refs], refs[n_tok_refs:]
    (sh_ref, sc_ref, g1_ref, win_ref, gqa_ref, wqb_ref, gkva_ref, wk_ref, wv_ref,
     gmq_ref, gmk_ref, ggq_ref, ggk_ref, rm_ref, rg_ref,
     qm_ref, km_ref, vm_ref, qg_ref, kg_ref, vg_ref) = refs
    x = _tok(x_refs, pl.program_id(0) < n_lat_batches)
    hparts = _norm_mod_parts(x, g1_ref[...], sh_ref[...], sc_ref[...])
    h = jnp.concatenate(hparts, axis=0)
    rope_m = (rm_ref[0], rm_ref[1], rm_ref[2])
    rope_g = (rg_ref[0], rg_ref[1], rg_ref[2])
    pw = 2 * LANES
    o_ckv = MLA_Q_RANK
    o_kr = o_ckv + MLA_KV_RANK
    o_gq = o_kr + LANES
    o_gk = o_gq + GQA_HEADS * LANES
    o_gv = o_gk + GQA_KV_HEADS * LANES
    two = lambda a: jnp.concatenate([a, a], axis=1)

    low = _rdot(hparts, win_ref[:, 0:o_gq])
    cqn = _rms(low[:, 0:o_ckv], gqa_ref[...], 1.0 / MLA_Q_RANK).astype(BF16)
    ckvn = _rms(low[:, o_ckv:o_kr], gkva_ref[...], 1.0 / MLA_KV_RANK).astype(BF16)
    krp = low[:, o_kr:o_gq]
    gmk = gmk_ref[...]
    kr_rot = two(_rope(krp * gmk[:, :LANES], *rope_m, MLA_ROPE // 4))
    kr_ss = _pair_sumsq(two(krp))
    vm_ref[...] = _dot(ckvn, wv_ref[...]).astype(BF16)
    vg_ref[...] = _dot(h, win_ref[:, o_gv:o_gv + GQA_KV_HEADS * LANES]).astype(BF16)

    def normed_roped(lhs, w_ref, c0, gain_ref, inv_n, tabs, shift, out_ref, p):
        blk = _dot(lhs, w_ref[:, c0 + p * pw:c0 + (p + 1) * pw])
        yield
        ss = _pair_sumsq(blk)
        yield
        y = blk * lax.rsqrt(ss * inv_n + EPS) * gain_ref[...]
        yield
        out_ref[:, p * pw:(p + 1) * pw] = _rope_pair(y, tabs, shift).astype(BF16)

    def mla_key(p):
        blk = _dot(ckvn, wk_ref[:, p * pw:(p + 1) * pw])
        yield
        ss = _pair_sumsq(blk)
        yield
        r = lax.rsqrt((ss + kr_ss) * (1.0 / MLA_QK) + EPS)
        yield
        km_ref[:, p * pw:(p + 1) * pw] = (r * (blk * gmk + kr_rot)).astype(BF16)

    for p in range(MLA_HEADS // 2):
        chains = [normed_roped(cqn, wqb_ref, 0, gmq_ref, 1.0 / MLA_QK, rope_m, MLA_ROPE // 4, qm_ref, p),
                  mla_key(p),
                  normed_roped(h, win_ref, o_gq, ggq_ref, 1.0 / GQA_DIM, rope_g, GQA_DIM // 4, qg_ref, p)]
        if p < GQA_KV_HEADS // 2:
            chains.append(normed_roped(h, win_ref, o_gk, ggk_ref, 1.0 / GQA_DIM, rope_g, GQA_DIM // 4, kg_ref, p))
        _lockstep(chains)


def _pad_cols(w, width):
    return jnp.pad(w, ((0, 0), (0, width - w.shape[1])))


def _even_weights(w_in, w_qb, w_kvb, g_mq, g_mk, g_gq, g_gk):
    d = w_in.shape[0]
    s0 = MLA_Q_RANK
    s1 = s0 + MLA_KV_RANK
    s2 = s1 + MLA_ROPE
    s3 = s2 + GQA_HEADS * GQA_DIM
    s4 = s3 + GQA_KV_HEADS * GQA_DIM
    cq, ckv, kr = w_in[:, :s0], w_in[:, s0:s1], w_in[:, s1:s2]
    gq, gk, gv = w_in[:, s2:s3], w_in[:, s3:s4], w_in[:, s4:]
    kr_p = jnp.pad(kr, ((0, 0), (MLA_NOPE, LANES - MLA_QK)))
    gq_p = jnp.pad(gq.reshape(d, GQA_HEADS, GQA_DIM), ((0, 0), (0, 0), (0, LANES - GQA_DIM))).reshape(d, -1)
    gk_p = jnp.pad(gk.reshape(d, GQA_KV_HEADS, GQA_DIM), ((0, 0), (0, 0), (0, LANES - GQA_DIM))).reshape(d, -1)
    gv3 = gv.reshape(d, GQA_KV_HEADS, GQA_DIM)
    gv_p = jnp.concatenate([gv3, gv3], axis=-1).reshape(d, -1)
    win = jnp.concatenate([cq, ckv, kr_p, gq_p, gk_p, gv_p], axis=1).astype(BF16)
    wqb = jnp.pad(w_qb.reshape(MLA_Q_RANK, MLA_HEADS, MLA_QK),
                  ((0, 0), (0, 0), (0, LANES - MLA_QK))).reshape(MLA_Q_RANK, -1).astype(BF16)
    kv3 = w_kvb.reshape(MLA_KV_RANK, MLA_HEADS, MLA_NOPE + MLA_V)
    wk = jnp.pad(kv3[:, :, :MLA_NOPE], ((0, 0), (0, 0), (0, LANES - MLA_NOPE))).reshape(MLA_KV_RANK, -1).astype(BF16)
    wv = kv3[:, :, MLA_NOPE:].reshape(MLA_KV_RANK, -1).astype(BF16)
    pair = lambda g: jnp.tile(_pad_cols(g[None, :], LANES), (1, 2))
    gmq = pair(g_mq * (MLA_QK ** -0.5 * LOG2E))
    gmk = pair(g_mk)
    ggq = pair(g_gq * (GQA_DIM ** -0.5 * LOG2E))
    ggk = pair(g_gk)
    return win, wqb, wk, wv, gmq, gmk, ggq, ggk


def _rope_tables(n_tokens, rot_dim, offset):
    rows = n_tokens // GRID_W
    row = np.repeat(np.arange(rows), GRID_W).astype(np.float32)
    col = np.tile(np.arange(GRID_W), rows).astype(np.float32)
    n_freq = rot_dim // 4
    inv = (np.float32(ROPE_THETA) ** (-np.arange(n_freq, dtype=np.float32) / np.float32(n_freq))).astype(np.float32)
    a_r, a_c = row[:, None] * inv, col[:, None] * inv
    ang = np.concatenate([a_r, a_r, a_c, a_c], axis=-1)
    cos, sin = np.cos(ang), np.sin(ang)
    quarter = (np.arange(rot_dim) // n_freq) % 2
    sa = np.where(quarter == 0, -sin, 0.0)
    sb = np.where(quarter == 1, sin, 0.0)
    pad = ((0, 0), (offset, LANES - offset - rot_dim))
    lat = np.stack([np.pad(cos, pad, constant_values=1.0), np.pad(sa, pad), np.pad(sb, pad)])
    ident = np.stack([np.ones((n_tokens, LANES)), np.zeros((n_tokens, LANES)), np.zeros((n_tokens, LANES))])
    return jnp.asarray(np.stack([lat, ident]), F32)


def _even_in(tok, sh, sc, g1, ew, gqa, gkva, rope_m, rope_g):
    nb, t, d = tok.nb, tok.t, tok.d
    win, wqb, wk, wv, gmq, gmk, ggq, ggk = ew
    tm = TM_IN
    full = lambda a: pl.BlockSpec(a.shape, lambda b, i: (0,) * a.ndim)
    row = pl.BlockSpec((None, 1, d), lambda b, i: (b, 0, 0))
    rope_spec = pl.BlockSpec((None, 3, tm, LANES), lambda b, i: (jnp.where(b == nb - 1, 1, 0), 0, i, 0))
    x_args, x_specs = tok.specs(tm, lambda i: i)

    def out(width):
        return jax.ShapeDtypeStruct((nb, t, width), BF16), pl.BlockSpec((None, tm, width), lambda b, i: (b, i, 0))

    outs = [out(MLA_HEADS * LANES), out(MLA_HEADS * LANES), out(MLA_HEADS * MLA_V),
            out(GQA_HEADS * LANES), out(GQA_KV_HEADS * LANES), out(GQA_KV_HEADS * LANES)]
    return pl.pallas_call(
        functools.partial(_even_in_kernel, n_tok_refs=tok.n_refs, n_lat_batches=nb - 1),
        out_shape=[o[0] for o in outs],
        grid=(nb, t // tm),
        in_specs=x_specs + [row, row, full(g1), full(win), full(gqa),
                            full(wqb), full(gkva), full(wk), full(wv), full(gmq), full(gmk), full(ggq), full(ggk),
                            rope_spec, rope_spec],
        out_specs=[o[1] for o in outs],
        compiler_params=_cparams(("parallel", "parallel")),
        name="even_in",
    )(*x_args, sh, sc, g1, win, gqa, wqb, gkva, wk, wv, gmq, gmk, ggq, ggk, rope_m, rope_g)


def _attn_kernel(q_ref, kl_ref, kc_ref, vl_ref, vc_ref, o_ref, s_scr, p_scr, *,
                 n_heads, k_share, v_share, ctx_len, n_lat_batches):
    tq = q_ref.shape[0]

    def run(q0, nq, parts):
        lane = lax.broadcasted_iota(jnp.int32, (nq, LANES), 1)
        chunks, off = [], 0
        for k_ref, _, base, n in parts:
            size = min(KV_CHUNK, n)
            for c in range(n // size):
                chunks.append((k_ref, base + c * size, size, off + c * size))
            off += n
        mx, m, lsum = {}, {}, {}
        outs = [None] * n_heads

        def scores(hd, ci):
            k_ref, r0, size, c0 = chunks[ci]
            kb = hd // k_share
            s = _dot_nt(q_ref[q0:q0 + nq, hd * LANES:(hd + 1) * LANES],
                        k_ref[pl.ds(r0, size), kb * LANES:(kb + 1) * LANES])
            s_scr[hd % 2, 0:nq, c0:c0 + size] = s
            for j in range(size // LANES):
                blk = s[:, j * LANES:(j + 1) * LANES]
                mx[hd] = blk if hd not in mx else jnp.maximum(mx[hd], blk)

        def values(hd, ci):
            _, _, size, c0 = chunks[ci]
            p = jnp.exp2(s_scr[hd % 2, 0:nq, c0:c0 + size] - m[hd])
            for j in range(size // LANES):
                blk = p[:, j * LANES:(j + 1) * LANES]
                lsum[hd] = blk if hd not in lsum else lsum[hd] + blk
            p_scr[hd % 2, 0:nq, c0:c0 + size] = p.astype(BF16)

        def product(hd):
            vb = hd // v_share
            out, off = None, 0
            for _, v_ref, base, n in parts:
                part = _dot(p_scr[hd % 2, 0:nq, off:off + n], v_ref[pl.ds(base, n), vb * LANES:(vb + 1) * LANES])
                out = part if out is None else out + part
                off += n
            return out

        for j in range(n_heads + 1):
            for ci in range(len(chunks)):
                if j < n_heads:
                    scores(j, ci)
                if j >= 1:
                    values(j - 1, ci)
            if j < n_heads:
                m[j] = jnp.max(mx.pop(j), axis=-1, keepdims=True)
            if j >= 1:
                outs[j - 1] = product(j - 1) * (1.0 / jnp.sum(lsum.pop(j - 1), axis=-1, keepdims=True))
        for j in range(n_heads // 2):
            pair = jnp.where(lane < LANES // 2, outs[2 * j], outs[2 * j + 1])
            o_ref[q0:q0 + nq, j * LANES:(j + 1) * LANES] = pair.astype(o_ref.dtype)

    b = pl.program_id(0)
    is_lat = b < n_lat_batches
    per_blk = kc_ref.shape[0] // ctx_len

    @pl.when(is_lat)
    def _():
        own = pl.multiple_of((b % per_blk) * ctx_len, ctx_len)
        run(0, tq, [(kc_ref, vc_ref, own, ctx_len), (kl_ref, vl_ref, 0, kl_ref.shape[0])])

    @pl.when(jnp.logical_not(is_lat))
    def _():
        for sub in range(tq // ctx_len):
            run(sub * ctx_len, ctx_len, [(kc_ref, vc_ref, sub * ctx_len, ctx_len)])


def _attention(Q, K, V, *, k_share, v_share, ctx_len):
    nb, t, qw = Q.shape
    nbl = nb - 1
    n_heads = qw // LANES
    assert TQ % ctx_len == 0 and nbl * ctx_len == t
    per_blk = TQ // ctx_len
    kw, vw, ow = K.shape[-1], V.shape[-1], qw // 2
    lat_b = lambda b: jnp.minimum(b, nbl - 1)
    ctx_blk = lambda b, i: jnp.where(b < nbl, b // per_blk, i)
    return pl.pallas_call(
        functools.partial(_attn_kernel, n_heads=n_heads, k_share=k_share, v_share=v_share, ctx_len=ctx_len,
                          n_lat_batches=nbl),
        out_shape=jax.ShapeDtypeStruct((nb, t, ow), BF16),
        grid=(nb, t // TQ),
        in_specs=[
            pl.BlockSpec((None, TQ, qw), lambda b, i: (b, i, 0)),
            pl.BlockSpec((None, t, kw), lambda b, i: (lat_b(b), 0, 0)),
            pl.BlockSpec((None, TQ, kw), lambda b, i: (nbl, ctx_blk(b, i), 0)),
            pl.BlockSpec((None, t, vw), lambda b, i: (lat_b(b), 0, 0)),
            pl.BlockSpec((None, TQ, vw), lambda b, i: (nbl, ctx_blk(b, i), 0)),
        ],
        out_specs=pl.BlockSpec((None, TQ, ow), lambda b, i: (b, i, 0)),
        scratch_shapes=[pltpu.VMEM((2, TQ, t + ctx_len), F32), pltpu.VMEM((2, TQ, t + ctx_len), BF16)],
        compiler_params=_cparams(("parallel", "arbitrary")),
        name="attn_k%d_v%d" % (k_share, v_share),
    )(Q, K, K, V, V)


def _ext_rows(main, prev, nxt):
    hp = prev.shape[0]
    return jnp.concatenate([prev.astype(F32)[hp - SUBLANES:hp], main.astype(F32), nxt.astype(F32)[0:SUBLANES]], axis=0)


def _mix_ffn_kernel(*refs, mixer, n_tok_refs, seq_len, ctx_len, n_lat_batches):
    is_latent = pl.program_id(0) < n_lat_batches
    x_parts = [_tok(refs[k * n_tok_refs:(k + 1) * n_tok_refs], is_latent) for k in range(3)]
    refs = refs[3 * n_tok_refs:]
    ext = lambda trio: _ext_rows(*(r[...] for r in trio))
    if mixer == "attn":
        oa_refs, ob_refs, refs = refs[:3], refs[3:6], refs[6:]
    else:
        hf_refs, hb_refs, o_refs, og_ref, refs = refs[:3], refs[3:6], refs[6:9], refs[9], refs[10:]
    (wo_ref, gt1_ref, sh_ref, sc_ref, gt2_ref, g2_ref, wu_ref, cw_ref, cb_ref, wd_ref, y_ref, a_scr) = refs[:12]
    tm = y_ref.shape[0]
    text = tm + 2 * SUBLANES
    mid = slice(SUBLANES, SUBLANES + tm)

    n_parts = _n_row_parts(text)
    rp = text // n_parts
    x_raw = _ext_rows(*x_parts)
    if mixer == "attn":
        oa = ext(oa_refs).astype(BF16)
        ob = ext(ob_refs).astype(BF16)
        na = oa.shape[1]
    else:
        m_scr = refs[12]
        hf, hb, og = ext(hf_refs), ext(hb_refs), ext(o_refs)
    xparts, hparts = [], []
    for k in range(n_parts):
        rows = slice(k * rp, (k + 1) * rp)
        if mixer == "attn":
            mix = _dot(oa[rows], wo_ref[0:na, :]) + _dot(ob[rows], wo_ref[na:, :])
        else:
            for hd in range(MLSTM_HEADS):
                cols = slice(hd * LANES, (hd + 1) * LANES)
                hn = _rms(hf[rows, cols] + hb[rows, cols], og_ref[:, cols], 1.0 / MLSTM_V)
                m_scr[rows, cols] = (hn * (1.0 / (1.0 + jnp.exp(-og[rows, cols])))).astype(BF16)
            mix = _dot(m_scr[rows, :], wo_ref[...])
        xk = x_raw[rows] + gt1_ref[...] * mix
        xparts.append(xk)
        hparts.append(_norm_mod(xk, g2_ref[...], sh_ref[...], sc_ref[...]).astype(BF16))
    xext = jnp.concatenate(xparts, axis=0)
    hext = jnp.concatenate(hparts, axis=0)
    has_prev, has_next = _edge_masks(pl.program_id(1) * tm, tm, is_latent, seq_len, ctx_len)
    for j in range(D_FF // FF_CHUNK):
        cols = slice(j * FF_CHUNK, (j + 1) * FF_CHUNK)
        lhs = hparts if j == 0 else hext
        gext = _rdot(lhs, wu_ref[:, cols])
        val = _rdot(lhs, wu_ref[:, D_FF + j * FF_CHUNK:D_FF + (j + 1) * FF_CHUNK])[mid]
        prev = jnp.where(has_prev, pltpu.roll(gext, 1, 0)[mid], 0.0)
        nxt = jnp.where(has_next, pltpu.roll(gext, text - 1, 0)[mid], 0.0)
        conv = prev * cw_ref[0:1, cols] + gext[mid] * cw_ref[1:2, cols] + nxt * cw_ref[2:3, cols] + cb_ref[:, cols]
        a_scr[:, cols] = (_silu(conv) * val).astype(BF16)
    y_ref[...] = xext[mid] + gt2_ref[...] * _dot(a_scr[...], wd_ref[...])


def _halo_blocks(t, tm, halo):
    per, nblk = tm // halo, t // halo
    return [(tm, lambda i: i), (halo, lambda i: jnp.maximum(i * per - 1, 0)),
            (halo, lambda i: jnp.minimum((i + 1) * per, nblk - 1))]


def _mix_ffn(mixer, tok, mix_in, out_g, w_out, gt1, sh, sc, gt2, g2, w_up, cw, cb, wd, *,
             n_out_batches, n_lat_batches, ctx_len):
    t, d = tok.t, tok.d
    tm = TM_FFN
    row = pl.BlockSpec((None, 1, d), lambda b, i: (b, 0, 0))
    full = lambda a: pl.BlockSpec(a.shape, lambda b, i: (0,) * a.ndim, pipeline_mode=pl.Buffered(1))
    args, specs = [], []
    for rows, blk in _halo_blocks(t, tm, SUBLANES):
        a_, s_ = tok.specs(rows, blk)
        args += a_
        specs += s_
    for a in mix_in:
        halo = SUBLANES * (4 // a.dtype.itemsize)
        for rows, blk in _halo_blocks(t, tm, halo):
            args.append(a)
            specs.append(pl.BlockSpec((None, rows, a.shape[-1]), lambda b, i, blk=blk: (b, blk(i), 0)))
    if mixer == "mlstm":
        args.append(out_g)
        specs.append(full(out_g))
    consts = [w_out, gt1, sh, sc, gt2, g2, w_up, cw, cb, wd]
    args += consts
    specs += [full(w_out), row, row, row, row] + [full(a) for a in consts[5:]]
    return pl.pallas_call(
        functools.partial(_mix_ffn_kernel, mixer=mixer, n_tok_refs=tok.n_refs, seq_len=t, ctx_len=ctx_len,
                          n_lat_batches=n_lat_batches),
        out_shape=jax.ShapeDtypeStruct((n_out_batches, t, d), F32),
        grid=(n_out_batches, t // tm),
        in_specs=specs,
        out_specs=pl.BlockSpec((None, tm, d), lambda b, i: (b, i, 0)),
        scratch_shapes=[pltpu.VMEM((tm, D_FF), BF16)]
        + ([pltpu.VMEM((tm + 2 * SUBLANES, d), BF16)] if mixer == "mlstm" else []),
        compiler_params=_cparams(("parallel", "parallel")),
        name=mixer + "_ffn",
    )(*args)


def _odd_in_kernel(x_ref, xp_ref, xn_ref, sh_ref, sc_ref, g1_ref, w_ref, gb_ref, cw_ref, cb_ref,
                   qk_ref, v_ref, o_ref, g_ref, *, seq_len, ctx_len, n_lat_batches):
    tm = x_ref.shape[0]
    text = tm + 2 * SUBLANES
    mid = slice(SUBLANES, SUBLANES + tm)
    xext = jnp.concatenate([xp_ref[...], x_ref[...], xn_ref[...]], axis=0)
    hparts = _norm_mod_parts(xext, g1_ref[...], sh_ref[...], sc_ref[...])
    hext = jnp.concatenate(hparts, axis=0)
    nqk = qk_ref.shape[1]
    nv = v_ref.shape[1]
    has_prev, has_next = _edge_masks(pl.program_id(1) * tm, tm, pl.program_id(0) < n_lat_batches, seq_len, ctx_len)
    step = 2 * LANES
    for j in range(nqk // step):
        cols = slice(j * step, (j + 1) * step)
        ext = _rdot(hparts if j == 0 else hext, w_ref[:, cols])
        prev = jnp.where(has_prev, pltpu.roll(ext, 1, 0)[mid], 0.0)
        nxt = jnp.where(has_next, pltpu.roll(ext, text - 1, 0)[mid], 0.0)
        conv = prev * cw_ref[0:1, cols] + ext[mid] * cw_ref[1:2, cols] + nxt * cw_ref[2:3, cols] + cb_ref[:, cols]
        scale = MLSTM_QK ** -0.5 if j >= nqk // (2 * step) else 1.0
        qk_ref[:, cols] = (_silu(conv) * scale).astype(BF16)
    v_ref[...] = _dot(hext, w_ref[:, nqk:nqk + nv])[mid].astype(BF16)
    o_ref[...] = _dot(hext, w_ref[:, nqk + nv:nqk + 2 * nv])[mid]
    g_ref[...] = _dot(hext, w_ref[:, nqk + 2 * nv:])[mid] + gb_ref[...]


def _odd_in(X, sh, sc, g1, w, gate_b, cw, cb, *, ctx_len):
    nb, t, d = X.shape
    tm = TM_ODD
    nqk = 2 * MLSTM_HEADS * MLSTM_QK
    nv = MLSTM_HEADS * MLSTM_V
    row = pl.BlockSpec((None, 1, d), lambda b, i: (b, 0, 0))
    full = lambda a: pl.BlockSpec(a.shape, lambda b, i: (0,) * a.ndim)
    tile = lambda wd: pl.BlockSpec((None, tm, wd), lambda b, i: (b, i, 0))
    x_specs = [pl.BlockSpec((None, rows, d), lambda b, i, blk=blk: (b, blk(i), 0))
               for rows, blk in _halo_blocks(t, tm, SUBLANES)]
    return pl.pallas_call(
        functools.partial(_odd_in_kernel, seq_len=t, ctx_len=ctx_len, n_lat_batches=nb - 1),
        out_shape=[jax.ShapeDtypeStruct((nb, t, nqk), BF16), jax.ShapeDtypeStruct((nb, t, nv), BF16),
                   jax.ShapeDtypeStruct((nb, t, nv), F32), jax.ShapeDtypeStruct((nb, t, LANES), F32)],
        grid=(nb, t // tm),
        in_specs=x_specs + [row, row, full(g1), full(w), full(gate_b), full(cw), full(cb)],
        out_specs=[tile(nqk), tile(nv), tile(nv), tile(LANES)],
        compiler_params=_cparams(("parallel", "parallel")),
        name="odd_in",
    )(X, X, X, sh, sc, g1, w, gate_b, cw, cb)


def _log_sigmoid(x):
    return jnp.minimum(x, 0.0) - jnp.log(1.0 + jnp.exp(-jnp.abs(x)))


def _running_max(x, rev):
    n = x.shape[0]
    row = lax.broadcasted_iota(jnp.int32, x.shape, 0)
    k = 1
    while k < n:
        if rev:
            x = jnp.where(row < n - k, jnp.maximum(x, pltpu.roll(x, n - k, 0)), x)
        else:
            x = jnp.where(row >= k, jnp.maximum(x, pltpu.roll(x, k, 0)), x)
        k *= 2
    return x


def _mlstm_kernel(qf_ref, kf_ref, vf_ref, gf_ref, qb_ref, kb_ref, vb_ref, gb_ref, hf_ref, hb_ref, cn_scr, m_scr):
    @pl.when(pl.program_id(1) == 0)
    def _():
        cn_scr[...] = jnp.zeros_like(cn_scr)
        m_scr[...] = jnp.zeros_like(m_scr)

    dirs = (0, 1)
    q_refs, k_refs, v_refs = (qf_ref, qb_ref), (kf_ref, kb_ref), (vf_ref, vb_ref)
    g_refs, h_refs = (gf_ref, gb_ref), (hf_ref, hb_ref)
    L = ML_CHUNK
    n_sub = qf_ref.shape[0] // L
    nh = MLSTM_HEADS
    i_off = (0, 2 * nh)
    last = (L - 1, 0)
    r_i = lax.broadcasted_iota(jnp.int32, (L, L), 0)
    c_i = lax.broadcasted_iota(jnp.int32, (L, L), 1)
    tri = (c_i <= r_i, c_i >= r_i)
    ones = jnp.ones((L, LANES), BF16)
    row = lax.broadcasted_iota(jnp.int32, (LANES, L), 0)
    hc = lambda hd: slice(hd * LANES, (hd + 1) * LANES)
    rows_of = lambda sub: [slice(sub * L, (sub + 1) * L), slice((n_sub - 1 - sub) * L, (n_sub - sub) * L)]

    m_prev = [m_scr[d] for d in dirs]
    gates = []
    for sub in range(n_sub):
        rows = rows_of(sub)
        g = [g_refs[d][rows[d], :] for d in dirs]
        b_f = [jnp.dot(tri[d].astype(F32), _log_sigmoid(g[d]), precision=lax.Precision.HIGHEST,
                       preferred_element_type=F32) for d in dirs]
        b = [pltpu.roll(b_f[d], LANES - nh, 1) for d in dirs]
        u = [g[d] - b[d] for d in dirs]
        cmax = [_running_max(u[d], d == 1) for d in dirs]
        mm = [jnp.maximum(m_prev[d], cmax[d]) for d in dirs]
        em = [jnp.exp(-(b[d] + mm[d])) for d in dirs]
        b_last = [b[d][last[d]:last[d] + 1, :] for d in dirs]
        a = [b_last[d] - b[d] + g[d] for d in dirs]
        m_new = [jnp.maximum(b_last[d] + m_prev[d], jnp.max(a[d], axis=0, keepdims=True)) for d in dirs]
        decay = [jnp.exp(b_last[d] + m_prev[d] - m_new[d]) for d in dirs]
        w = [jnp.exp(a[d] - m_new[d]) for d in dirs]
        u_t = [u[d].T for d in dirs]
        w_t = [w[d].T for d in dirs]
        gates.append((rows, m_prev, mm, em, decay, u_t, w_t))
        m_prev = m_new
    for d in dirs:
        m_scr[d] = m_prev[d]

    for sub in range(n_sub):
        rows, m_prev, mm, em, decay, u_t, w_t = gates[sub]
        for pair in range(nh // 2):
            pc = slice(pair * LANES, (pair + 1) * LANES)
            qp = [q_refs[d][rows[d], pc] for d in dirs]
            kt = [k_refs[d][rows[d], pc].astype(F32).T for d in dirs]
            ch = [(d, 2 * pair + half, i_off[d] + 2 * pair + half, half) for half in range(2) for d in dirs]
            kt_h = [jnp.where((row < MLSTM_QK) if half == 0 else (row >= MLSTM_QK), kt[d], 0.0)
                    for d, hd, col, half in ch]
            v1 = [jnp.concatenate([v_refs[d][rows[d], hc(hd)], ones], axis=1) for d, hd, col, half in ch]
            mm_b = [jnp.broadcast_to(mm[d][:, col:col + 1], (L, LANES)) for d, hd, col, half in ch]
            em_b = [jnp.broadcast_to(em[d][:, col:col + 1], (L, LANES)) for d, hd, col, half in ch]
            dw = [jnp.where(tri[d], jnp.exp(u_t[d][col:col + 1, :] - mm_b[i]), 0.0) for i, (d, hd, col, half) in enumerate(ch)]
            s = [_dot(qp[d], kt_h[i].astype(BF16)) * dw[i] for i, (d, hd, col, half) in enumerate(ch)]
            cn = [cn_scr[d, hd] for d, hd, col, half in ch]
            qc = [_dot(qp[d], cn[i].astype(BF16)) for i, (d, hd, col, half) in enumerate(ch)]
            sv = [_dot(s[i].astype(BF16), v1[i]) for i in range(len(ch))]
            w_inter = [jnp.exp(m_prev[d][:, col:col + 1] - mm_b[i]) for i, (d, hd, col, half) in enumerate(ch)]
            num = [w_inter[i] * qc[i][:, :LANES] + sv[i][:, :LANES] for i in range(len(ch))]
            den = [w_inter[i] * qc[i][:, LANES:] + sv[i][:, LANES:] for i in range(len(ch))]
            for i, (d, hd, col, half) in enumerate(ch):
                h_refs[d][rows[d], hc(hd)] = num[i] * (1.0 / jnp.maximum(jnp.abs(den[i]), em_b[i]))
            ktw = [(kt_h[i] * w_t[d][col:col + 1, :]).astype(BF16) for i, (d, hd, col, half) in enumerate(ch)]
            upd = [_dot(ktw[i], v1[i]) for i in range(len(ch))]
            for i, (d, hd, col, half) in enumerate(ch):
                cn_scr[d, hd] = decay[d][:, col:col + 1] * cn[i] + upd[i]


def _mlstm(QK, V, G, *, ctx_len):
    nb, t, nqk = QK.shape
    nbl = nb - 1
    L = ML_CHUNK * ML_CHUNKS_PER_STEP
    assert ctx_len % L == 0
    ncc, nlc = ctx_len // L, t // L
    nq = nqk // 2
    nv = V.shape[-1]

    def spec(width, col, rev):
        def index(b, c):
            cc, lc = (ncc - 1 - c, nlc - 1 - (c - ncc)) if rev else (c, c - ncc)
            is_ctx = c < ncc
            return jnp.where(is_ctx, nbl, b), jnp.where(is_ctx, b * ncc + cc, lc), col
        return pl.BlockSpec((None, L, width), index)

    ins = lambda rev: [spec(nq, 0, rev), spec(nq, 1, rev), spec(nv, 0, rev), spec(LANES, 0, rev)]
    out = jax.ShapeDtypeStruct((nb, t, nv), F32)
    return pl.pallas_call(
        _mlstm_kernel,
        out_shape=[out, out],
        grid=(nbl, ncc + nlc),
        in_specs=ins(False) + ins(True),
        out_specs=[spec(nv, 0, False), spec(nv, 0, True)],
        scratch_shapes=[pltpu.VMEM((2, MLSTM_HEADS, LANES, MLSTM_V + LANES), F32),
                        pltpu.VMEM((2, 1, LANES), F32)],
        compiler_params=_cparams(("parallel", "arbitrary")),
        name="mlstm",
    )(QK, QK, V, G, QK, QK, V, G)


def kernel(x, c, ctx, c_ctx, ada_w, ada_b, norm1_g, norm2_g, ffn_w_up, ffn_conv_w, ffn_conv_b, ffn_w_down,
           att_w_in, mla_qa_g, mla_w_qb, mla_kva_g, mla_w_kvb, mla_q_g, mla_k_g, gqa_q_g, gqa_k_g, att_w_out,
           ml_w_in, ml_conv_w, ml_conv_b, ml_gate_b, ml_out_g, ml_w_out):
    B, T, D = x.shape
    ctx_len = ctx.shape[1]
    depth = ada_w.shape[0]
    assert B * ctx_len == T, "context sequences must tile one latent-length row of the token array"
    NB = B + 1

    tok = _Tokens(x, ctx.reshape(1, T, D))

    rows = -(-NB // SUBLANES) * SUBLANES
    cvec = jnp.zeros((rows, D), F32).at[:B].set(c).at[B].set(c_ctx)
    mod = _adaln(cvec, ada_w, ada_b)
    mod = mod.reshape(depth, rows, 6, 1, D).transpose(0, 2, 1, 3, 4)

    rope_m = _rope_tables(T, MLA_ROPE, MLA_NOPE)
    rope_g = _rope_tables(T, GQA_DIM, 0)

    for layer in range(depth):
        last = layer == depth - 1
        j = layer // 2
        sh1, sc1, gt1, sh2, sc2, gt2 = (mod[layer, i] for i in range(6))
        g1 = norm1_g[layer][None, :]
        g2 = norm2_g[layer][None, :]
        if layer % 2 == 0:
            ew = _even_weights(att_w_in[j], mla_w_qb[j], mla_w_kvb[j], mla_q_g[j], mla_k_g[j],
                               gqa_q_g[j], gqa_k_g[j])
            qm, km, vm, qg, kg, vg = _even_in(tok, sh1, sc1, g1, ew, mla_qa_g[j][None, :], mla_kva_g[j][None, :],
                                              rope_m, rope_g)
            oa = _attention(qm, km, vm, k_share=1, v_share=2, ctx_len=ctx_len)
            ob = _attention(qg, kg, vg, k_share=GQA_HEADS // GQA_KV_HEADS, v_share=GQA_HEADS // GQA_KV_HEADS,
                            ctx_len=ctx_len)
            mixer, mix_in, out_g, w_out = "attn", (oa, ob), None, att_w_out[j]
        else:
            assert tok.ctx is None, "an odd layer reads the assembled token array"
            gate_b = _pad_cols(ml_gate_b[j][None, :], LANES)
            w_in = jnp.pad(ml_w_in[j], ((0, 0), (0, LANES - 4 * MLSTM_HEADS))).astype(BF16)
            qk, v, o, gates = _odd_in(tok.lat, sh1, sc1, g1, w_in, gate_b, ml_conv_w[j], ml_conv_b[j][None, :],
                                      ctx_len=ctx_len)
            hf, hb = _mlstm(qk, v, gates, ctx_len=ctx_len)
            mixer, mix_in, out_g, w_out = "mlstm", (hf, hb, o), ml_out_g[j].reshape(1, -1), ml_w_out[j]
        X = _mix_ffn(mixer, tok, mix_in, out_g, w_out.astype(BF16), gt1, sh2, sc2, gt2, g2,
                     ffn_w_up[layer].astype(BF16), ffn_conv_w[layer], ffn_conv_b[layer][None, :],
                     ffn_w_down[layer].astype(BF16),
                     n_out_batches=B if last else NB, n_lat_batches=B, ctx_len=ctx_len)
        tok = _Tokens(X)
    return X[:B]
```

```python
import functools

import numpy as np
import jax
import jax.numpy as jnp
from jax import lax
from jax.experimental import pallas as pl
from jax.experimental.pallas import tpu as pltpu

F32 = jnp.float32
BF16 = jnp.bfloat16

EPS = 1e-6
ROPE_THETA = 10000.0
GRID_W = 64
LOG2E = 1.4426950408889634

MLA_HEADS = 8
MLA_Q_RANK = 384
MLA_KV_RANK = 256
MLA_NOPE = 64
MLA_ROPE = 32
MLA_V = 64
MLA_QK = MLA_NOPE + MLA_ROPE
GQA_HEADS = 8
GQA_KV_HEADS = 2
GQA_DIM = 64
MLSTM_HEADS = 8
MLSTM_QK = 64
MLSTM_V = 128
D_FF = 2816

LANES = 128
SUBLANES = 8
VMEM_LIMIT = 56 * 1024 * 1024

TM_IN = 512
TM_ODD = 512
TM_FFN = 512
FF_CHUNK = 256
TQ = 512
KV_CHUNK = 512
ML_CHUNK = 128
ML_CHUNKS_PER_STEP = 2


def _cparams(sem):
    return pltpu.CompilerParams(dimension_semantics=sem, vmem_limit_bytes=VMEM_LIMIT)


def _dot(a, b):
    return jnp.dot(a, b, preferred_element_type=F32)


def _dot_nt(a, b):
    return lax.dot_general(a, b, (((1,), (1,)), ((), ())), preferred_element_type=F32)


def _rms(x, gain_row, inv_n):
    ss = jnp.sum(x * x, axis=-1, keepdims=True) * inv_n
    return x * lax.rsqrt(ss + EPS) * gain_row


def _norm_mod(x, g_row, sh_row, sc_row):
    ms = jnp.mean(x * x, axis=-1, keepdims=True)
    return x * lax.rsqrt(ms + EPS) * (g_row * (1.0 + sc_row)) + sh_row


def _n_row_parts(rows):
    tiles = rows // (2 * SUBLANES)
    assert tiles * 2 * SUBLANES == rows
    return next(n for n in (3, 5, 2, 1) if tiles % n == 0)


def _norm_mod_parts(x, g_row, sh_row, sc_row):
    n_parts = _n_row_parts(x.shape[0])
    rp = x.shape[0] // n_parts
    return [_norm_mod(x[k * rp:(k + 1) * rp], g_row, sh_row, sc_row).astype(BF16) for k in range(n_parts)]


def _rdot(lhs, w):
    if isinstance(lhs, (list, tuple)):
        return jnp.concatenate([_dot(p, w) for p in lhs], axis=0)
    return _dot(lhs, w)


def _silu(x):
    return x * (1.0 / (1.0 + jnp.exp(-x)))


def _edge_masks(row0, n_rows, is_latent, seq_len, ctx_len):
    assert seq_len & (seq_len - 1) == 0 and ctx_len & (ctx_len - 1) == 0, "sequence lengths must be powers of two"
    last = jnp.where(is_latent, seq_len - 1, ctx_len - 1)
    pos = (row0 + lax.broadcasted_iota(jnp.int32, (n_rows, 1), 0)) & last
    return pos != 0, pos != last


def _lockstep(chains):
    chains = list(chains)
    while chains:
        for c in list(chains):
            try:
                next(c)
            except StopIteration:
                chains.remove(c)


class _Tokens:
    def __init__(self, lat, ctx=None):
        self.lat, self.ctx = lat, ctx
        self.nb = lat.shape[0] + (0 if ctx is None else 1)
        self.t, self.d = lat.shape[1], lat.shape[2]
        self.n_refs = 1 if ctx is None else 2

    def specs(self, rows, blk):
        if self.ctx is None:
            return [self.lat], [pl.BlockSpec((None, rows, self.d), lambda b, i: (b, blk(i), 0))]
        nbl = self.nb - 1
        lat = pl.BlockSpec((None, rows, self.d), lambda b, i: (jnp.minimum(b, nbl - 1), jnp.where(b < nbl, blk(i), 0), 0))
        ctx = pl.BlockSpec((None, rows, self.d), lambda b, i: (0, jnp.where(b < nbl, 0, blk(i)), 0))
        return [self.lat, self.ctx], [lat, ctx]


def _tok(refs, is_latent):
    if len(refs) == 1:
        return refs[0][...]
    return jnp.where(is_latent, refs[0][...], refs[1][...])


def _adaln_kernel(c_ref, w_ref, b_ref, o_ref):
    c = _silu(c_ref[...]).astype(BF16)
    o_ref[...] = _dot(c, w_ref[...].astype(BF16)) + b_ref[...]


def _adaln(cvec, ada_w, ada_b):
    depth, d, n = ada_w.shape
    rows = cvec.shape[0]
    tn = 1536
    return pl.pallas_call(
        _adaln_kernel,
        out_shape=jax.ShapeDtypeStruct((depth, rows, n), F32),
        grid=(depth, n // tn),
        in_specs=[
            pl.BlockSpec((rows, d), lambda l, j: (0, 0)),
            pl.BlockSpec((None, d, tn), lambda l, j: (l, 0, j)),
            pl.BlockSpec((None, 1, tn), lambda l, j: (l, 0, j)),
        ],
        out_specs=pl.BlockSpec((None, rows, tn), lambda l, j: (l, 0, j)),
        compiler_params=_cparams(("arbitrary", "arbitrary")),
        name="adaln",
    )(cvec, ada_w, ada_b.reshape(depth, 1, n))


def _rope(y, cos, sa, sb, shift):
    return y * cos + pltpu.roll(y, LANES - shift, 1) * sa + pltpu.roll(y, shift, 1) * sb


def _group_sumsq(y, width=LANES):
    n = y.shape[1]
    r = lax.broadcasted_iota(jnp.int32, (n, n), 0) // width
    c = lax.broadcasted_iota(jnp.int32, (n, n), 1) // width
    return _dot((y * y).astype(BF16), jnp.where(r == c, 1.0, 0.0).astype(BF16))


def _rope_pair(y, tabs, shift):
    cos, sa, sb = tabs
    return jnp.concatenate([_rope(y[:, :LANES], cos, sa, sb, shift), _rope(y[:, LANES:], cos, sa, sb, shift)], axis=1)


def _even_in_kernel(*refs, n_tok_refs, n_lat_batches):
    x_refs, refs = refs[:n_tok_refs], refs[n_tok_refs:]
    (sh_ref, sc_ref, g1_ref, win_ref, gqa_ref, wqb_ref, gkva_ref, wk_ref, wv_ref,
     gmq_ref, gmk_ref, ggq_ref, ggk_ref, rm_ref, rg_ref,
     qm_ref, km_ref, vm_ref, qg_ref, kg_ref, vg_ref) = refs
    x = _tok(x_refs, pl.program_id(0) < n_lat_batches)
    hparts = _norm_mod_parts(x, g1_ref[...], sh_ref[...], sc_ref[...])
    h = jnp.concatenate(hparts, axis=0)
    rope_m = (rm_ref[0], rm_ref[1], rm_ref[2])
    rope_g = (rg_ref[0], rg_ref[1], rg_ref[2])
    pw = 2 * LANES
    o_ckv = MLA_Q_RANK
    o_kr = o_ckv + MLA_KV_RANK
    o_gq = o_kr + LANES
    o_gk = o_gq + GQA_HEADS * GQA_DIM
    o_gv = o_gk + GQA_KV_HEADS * LANES
    two = lambda a: jnp.concatenate([a, a], axis=1)

    low = _rdot(hparts, win_ref[:, 0:o_gq])
    cqn = _rms(low[:, 0:o_ckv], gqa_ref[...], 1.0 / MLA_Q_RANK).astype(BF16)
    ckvn = _rms(low[:, o_ckv:o_kr], gkva_ref[...], 1.0 / MLA_KV_RANK).astype(BF16)
    krp = low[:, o_kr:o_gq]
    gmk = gmk_ref[...]
    kr_rot = two(_rope(krp * gmk[:, :LANES], *rope_m, MLA_ROPE // 4))
    kr_ss = _group_sumsq(two(krp))
    vm_ref[...] = _dot(ckvn, wv_ref[...]).astype(BF16)
    vg_ref[...] = _dot(h, win_ref[:, o_gv:o_gv + GQA_KV_HEADS * LANES]).astype(BF16)

    def mla_query(p):
        blk = _dot(cqn, wqb_ref[:, p * pw:(p + 1) * pw])
        yield
        ss = _group_sumsq(blk)
        yield
        y = blk * lax.rsqrt(ss * (1.0 / MLA_QK) + EPS) * gmq_ref[...]
        yield
        qm_ref[:, p * pw:(p + 1) * pw] = _rope_pair(y, rope_m, MLA_ROPE // 4).astype(BF16)

    def mla_key(p):
        blk = _dot(ckvn, wk_ref[:, p * pw:(p + 1) * pw])
        yield
        ss = _group_sumsq(blk)
        yield
        r = lax.rsqrt((ss + kr_ss) * (1.0 / MLA_QK) + EPS)
        yield
        km_ref[:, p * pw:(p + 1) * pw] = (r * (blk * gmk + kr_rot)).astype(BF16)

    def gqa_query(p):
        blk = _dot(h, win_ref[:, o_gq + p * pw:o_gq + (p + 1) * pw])
        yield
        ss = _group_sumsq(blk, GQA_DIM)
        yield
        y = blk * lax.rsqrt(ss * (1.0 / GQA_DIM) + EPS) * ggq_ref[...]
        yield
        qg_ref[:, p * pw:(p + 1) * pw] = _rope_pair(y, rope_g, GQA_DIM // 4).astype(BF16)

    def gqa_key(p):
        blk = _dot(h, win_ref[:, o_gk + p * pw:o_gk + (p + 1) * pw])
        yield
        ss = _group_sumsq(blk)
        yield
        y = blk * lax.rsqrt(ss * (1.0 / GQA_DIM) + EPS) * ggk_ref[...]
        yield
        k2 = _rope_pair(y, rope_g, GQA_DIM // 4)
        for i in range(2):
            lo = k2[:, i * LANES:(i + 1) * LANES]
            c0 = (2 * p + i) * pw
            kg_ref[:, c0:c0 + LANES] = lo.astype(BF16)
            kg_ref[:, c0 + LANES:c0 + pw] = pltpu.roll(lo, GQA_DIM, 1).astype(BF16)

    for p in range(MLA_HEADS // 2):
        chains = [mla_query(p), mla_key(p)]
        if p < GQA_HEADS * GQA_DIM // pw:
            chains.append(gqa_query(p))
        if p < GQA_KV_HEADS // 2:
            chains.append(gqa_key(p))
        _lockstep(chains)


def _pad_cols(w, width):
    return jnp.pad(w, ((0, 0), (0, width - w.shape[1])))


def _even_weights(w_in, w_qb, w_kvb, g_mq, g_mk, g_gq, g_gk):
    w_in, w_qb, w_kvb = w_in.astype(BF16), w_qb.astype(BF16), w_kvb.astype(BF16)
    d = w_in.shape[0]
    s0 = MLA_Q_RANK
    s1 = s0 + MLA_KV_RANK
    s2 = s1 + MLA_ROPE
    s3 = s2 + GQA_HEADS * GQA_DIM
    s4 = s3 + GQA_KV_HEADS * GQA_DIM
    cq, ckv, kr = w_in[:, :s0], w_in[:, s0:s1], w_in[:, s1:s2]
    gq, gk, gv = w_in[:, s2:s3], w_in[:, s3:s4], w_in[:, s4:]
    kr_p = jnp.pad(kr, ((0, 0), (MLA_NOPE, LANES - MLA_QK)))
    gk_p = jnp.pad(gk.reshape(d, GQA_KV_HEADS, GQA_DIM), ((0, 0), (0, 0), (0, LANES - GQA_DIM))).reshape(d, -1)
    gv3 = gv.reshape(d, GQA_KV_HEADS, GQA_DIM)
    gv_p = jnp.concatenate([gv3, gv3], axis=-1).reshape(d, -1)
    win = jnp.concatenate([cq, ckv, kr_p, gq, gk_p, gv_p], axis=1)
    wqb = jnp.pad(w_qb.reshape(MLA_Q_RANK, MLA_HEADS, MLA_QK),
                  ((0, 0), (0, 0), (0, LANES - MLA_QK))).reshape(MLA_Q_RANK, -1)
    kv3 = w_kvb.reshape(MLA_KV_RANK, MLA_HEADS, MLA_NOPE + MLA_V)
    wk = jnp.pad(kv3[:, :, :MLA_NOPE], ((0, 0), (0, 0), (0, LANES - MLA_NOPE))).reshape(MLA_KV_RANK, -1)
    wv = kv3[:, :, MLA_NOPE:].reshape(MLA_KV_RANK, -1)
    pair = lambda g: jnp.tile(_pad_cols(g[None, :], LANES), (1, 2))
    gmq = pair(g_mq * (MLA_QK ** -0.5 * LOG2E))
    gmk = pair(g_mk)
    ggq = jnp.tile((g_gq * (GQA_DIM ** -0.5 * LOG2E))[None, :], (1, 2 * LANES // GQA_DIM))
    ggk = pair(g_gk)
    return win, wqb, wk, wv, gmq, gmk, ggq, ggk


def _rope_tables(n_tokens, rot_dim, offset, period=LANES):
    rows = n_tokens // GRID_W
    row = np.repeat(np.arange(rows), GRID_W).astype(np.float32)
    col = np.tile(np.arange(GRID_W), rows).astype(np.float32)
    n_freq = rot_dim // 4
    inv = (np.float32(ROPE_THETA) ** (-np.arange(n_freq, dtype=np.float32) / np.float32(n_freq))).astype(np.float32)
    a_r, a_c = row[:, None] * inv, col[:, None] * inv
    ang = np.concatenate([a_r, a_r, a_c, a_c], axis=-1)
    cos, sin = np.cos(ang), np.sin(ang)
    quarter = (np.arange(rot_dim) // n_freq) % 2
    sa = np.where(quarter == 0, -sin, 0.0)
    sb = np.where(quarter == 1, sin, 0.0)
    pad = ((0, 0), (offset, period - offset - rot_dim))
    rep = lambda a: np.tile(a, (1, LANES // period))
    lat = np.stack([rep(np.pad(cos, pad, constant_values=1.0)), rep(np.pad(sa, pad)), rep(np.pad(sb, pad))])
    ident = np.stack([np.ones((n_tokens, LANES)), np.zeros((n_tokens, LANES)), np.zeros((n_tokens, LANES))])
    return jnp.asarray(np.stack([lat, ident]), F32)


def _even_in(tok, sh, sc, g1, ew, gqa, gkva, rope_m, rope_g):
    nb, t, d = tok.nb, tok.t, tok.d
    win, wqb, wk, wv, gmq, gmk, ggq, ggk = ew
    tm = TM_IN
    full = lambda a: pl.BlockSpec(a.shape, lambda b, i: (0,) * a.ndim)
    row = pl.BlockSpec((None, 1, d), lambda b, i: (b, 0, 0))
    rope_spec = pl.BlockSpec((None, 3, tm, LANES), lambda b, i: (jnp.where(b == nb - 1, 1, 0), 0, i, 0))
    x_args, x_specs = tok.specs(tm, lambda i: i)

    def out(width):
        return jax.ShapeDtypeStruct((nb, t, width), BF16), pl.BlockSpec((None, tm, width), lambda b, i: (b, i, 0))

    outs = [out(MLA_HEADS * LANES), out(MLA_HEADS * LANES), out(MLA_HEADS * MLA_V),
            out(GQA_HEADS * GQA_DIM), out(GQA_KV_HEADS * 2 * LANES), out(GQA_KV_HEADS * LANES)]
    return pl.pallas_call(
        functools.partial(_even_in_kernel, n_tok_refs=tok.n_refs, n_lat_batches=nb - 1),
        out_shape=[o[0] for o in outs],
        grid=(nb, t // tm),
        in_specs=x_specs + [row, row, full(g1), full(win), full(gqa),
                            full(wqb), full(gkva), full(wk), full(wv), full(gmq), full(gmk), full(ggq), full(ggk),
                            rope_spec, rope_spec],
        out_specs=[o[1] for o in outs],
        compiler_params=_cparams(("parallel", "parallel")),
        name="even_in",
    )(*x_args, sh, sc, g1, win, gqa, wqb, gkva, wk, wv, gmq, gmk, ggq, ggk, rope_m, rope_g)


def _attn_kernel(q_ref, kl_ref, kc_ref, vl_ref, vc_ref, o_ref, s_scr, p_scr, *,
                 n_heads, q_pack, k_share, k_variants, v_share, ctx_len, n_lat_batches):
    tq = q_ref.shape[0]

    def run(q0, nq, parts):
        lane = lax.broadcasted_iota(jnp.int32, (nq, LANES), 1)
        chunks, off = [], 0
        for k_ref, _, base, n in parts:
            size = min(KV_CHUNK, n)
            for c in range(n // size):
                chunks.append((k_ref, base + c * size, size, off + c * size))
            off += n
        mx, m, lsum = {}, {}, {}
        outs = [None] * n_heads

        def scores(hd, ci):
            k_ref, r0, size, c0 = chunks[ci]
            qb = hd // q_pack
            kb = (hd // k_share) * k_variants + hd % k_variants
            s = _dot_nt(q_ref[q0:q0 + nq, qb * LANES:(qb + 1) * LANES],
                        k_ref[pl.ds(r0, size), kb * LANES:(kb + 1) * LANES])
            s_scr[hd % 2, 0:nq, c0:c0 + size] = s
            for j in range(size // LANES):
                blk = s[:, j * LANES:(j + 1) * LANES]
                mx[hd] = blk if hd not in mx else jnp.maximum(mx[hd], blk)

        def values(hd, ci):
            _, _, size, c0 = chunks[ci]
            p = jnp.exp2(s_scr[hd % 2, 0:nq, c0:c0 + size] - m[hd])
            for j in range(size // LANES):
                blk = p[:, j * LANES:(j + 1) * LANES]
                lsum[hd] = blk if hd not in lsum else lsum[hd] + blk
            p_scr[hd % 2, 0:nq, c0:c0 + size] = p.astype(BF16)

        def product(hd):
            vb = hd // v_share
            out, off = None, 0
            for _, v_ref, base, n in parts:
                part = _dot(p_scr[hd % 2, 0:nq, off:off + n], v_ref[pl.ds(base, n), vb * LANES:(vb + 1) * LANES])
                out = part if out is None else out + part
                off += n
            return out

        for j in range(n_heads + 1):
            for ci in range(len(chunks)):
                if j < n_heads:
                    scores(j, ci)
                if j >= 1:
                    values(j - 1, ci)
            if j < n_heads:
                m[j] = jnp.max(mx.pop(j), axis=-1, keepdims=True)
            if j >= 1:
                outs[j - 1] = product(j - 1) * (1.0 / jnp.sum(lsum.pop(j - 1), axis=-1, keepdims=True))
        for j in range(n_heads // 2):
            pair = jnp.where(lane < LANES // 2, outs[2 * j], outs[2 * j + 1])
            o_ref[q0:q0 + nq, j * LANES:(j + 1) * LANES] = pair.astype(o_ref.dtype)

    b = pl.program_id(0)
    is_lat = b < n_lat_batches
    per_blk = kc_ref.shape[0] // ctx_len

    @pl.when(is_lat)
    def _():
        own = pl.multiple_of((b % per_blk) * ctx_len, ctx_len)
        run(0, tq, [(kc_ref, vc_ref, own, ctx_len), (kl_ref, vl_ref, 0, kl_ref.shape[0])])

    @pl.when(jnp.logical_not(is_lat))
    def _():
        for sub in range(tq // ctx_len):
            run(sub * ctx_len, ctx_len, [(kc_ref, vc_ref, sub * ctx_len, ctx_len)])


def _attention(Q, K, V, *, n_heads, q_pack, k_share, k_variants, v_share, ctx_len):
    nb, t, qw = Q.shape
    nbl = nb - 1
    assert TQ % ctx_len == 0 and nbl * ctx_len == t
    per_blk = TQ // ctx_len
    kw, vw, ow = K.shape[-1], V.shape[-1], n_heads * LANES // 2
    lat_b = lambda b: jnp.minimum(b, nbl - 1)
    ctx_blk = lambda b, i: jnp.where(b < nbl, b // per_blk, i)
    return pl.pallas_call(
        functools.partial(_attn_kernel, n_heads=n_heads, q_pack=q_pack, k_share=k_share, k_variants=k_variants,
                          v_share=v_share, ctx_len=ctx_len, n_lat_batches=nbl),
        out_shape=jax.ShapeDtypeStruct((nb, t, ow), BF16),
        grid=(nb, t // TQ),
        in_specs=[
            pl.BlockSpec((None, TQ, qw), lambda b, i: (b, i, 0)),
            pl.BlockSpec((None, t, kw), lambda b, i: (lat_b(b), 0, 0)),
            pl.BlockSpec((None, TQ, kw), lambda b, i: (nbl, ctx_blk(b, i), 0)),
            pl.BlockSpec((None, t, vw), lambda b, i: (lat_b(b), 0, 0)),
            pl.BlockSpec((None, TQ, vw), lambda b, i: (nbl, ctx_blk(b, i), 0)),
        ],
        out_specs=pl.BlockSpec((None, TQ, ow), lambda b, i: (b, i, 0)),
        scratch_shapes=[pltpu.VMEM((2, TQ, t + ctx_len), F32), pltpu.VMEM((2, TQ, t + ctx_len), BF16)],
        compiler_params=_cparams(("parallel", "arbitrary")),
        name="attn_q%d_k%d" % (q_pack, k_share),
    )(Q, K, K, V, V)


def _ext_rows(main, prev, nxt):
    hp = prev.shape[0]
    return jnp.concatenate([prev.astype(F32)[hp - SUBLANES:hp], main.astype(F32), nxt.astype(F32)[0:SUBLANES]], axis=0)


def _mix_ffn_kernel(*refs, mixer, n_tok_refs, seq_len, ctx_len, n_lat_batches):
    is_latent = pl.program_id(0) < n_lat_batches
    x_parts = [_tok(refs[k * n_tok_refs:(k + 1) * n_tok_refs], is_latent) for k in range(3)]
    refs = refs[3 * n_tok_refs:]
    ext = lambda trio: _ext_rows(*(r[...] for r in trio))
    if mixer == "attn":
        oa_refs, ob_refs, refs = refs[:3], refs[3:6], refs[6:]
    else:
        hf_refs, hb_refs, o_refs, og_ref, refs = refs[:3], refs[3:6], refs[6:9], refs[9], refs[10:]
    (wo_ref, gt1_ref, sh_ref, sc_ref, gt2_ref, g2_ref, wu_ref, cw_ref, cb_ref, wd_ref, y_ref, a_scr) = refs[:12]
    tm = y_ref.shape[0]
    text = tm + 2 * SUBLANES
    mid = slice(SUBLANES, SUBLANES + tm)

    n_parts = _n_row_parts(text)
    rp = text // n_parts
    x_raw = _ext_rows(*x_parts)
    if mixer == "attn":
        oa = ext(oa_refs).astype(BF16)
        ob = ext(ob_refs).astype(BF16)
        na = oa.shape[1]
    else:
        m_scr = refs[12]
        hf, hb, og = ext(hf_refs), ext(hb_refs), ext(o_refs)
    xparts, hparts = [], []
    for k in range(n_parts):
        rows = slice(k * rp, (k + 1) * rp)
        if mixer == "attn":
            mix = _dot(oa[rows], wo_ref[0:na, :]) + _dot(ob[rows], wo_ref[na:, :])
        else:
            for hd in range(MLSTM_HEADS):
                cols = slice(hd * LANES, (hd + 1) * LANES)
                hn = _rms(hf[rows, cols] + hb[rows, cols], og_ref[:, cols], 1.0 / MLSTM_V)
                m_scr[rows, cols] = (hn * (1.0 / (1.0 + jnp.exp(-og[rows, cols])))).astype(BF16)
            mix = _dot(m_scr[rows, :], wo_ref[...])
        xk = x_raw[rows] + gt1_ref[...] * mix
        xparts.append(xk)
        hparts.append(_norm_mod(xk, g2_ref[...], sh_ref[...], sc_ref[...]).astype(BF16))
    xext = jnp.concatenate(xparts, axis=0)
    hext = jnp.concatenate(hparts, axis=0)
    has_prev, has_next = _edge_masks(pl.program_id(1) * tm, tm, is_latent, seq_len, ctx_len)
    for j in range(D_FF // FF_CHUNK):
        cols = slice(j * FF_CHUNK, (j + 1) * FF_CHUNK)
        lhs = hparts if j == 0 else hext
        gext = _rdot(lhs, wu_ref[:, cols])
        val = _rdot(lhs, wu_ref[:, D_FF + j * FF_CHUNK:D_FF + (j + 1) * FF_CHUNK])[mid]
        prev = jnp.where(has_prev, pltpu.roll(gext, 1, 0)[mid], 0.0)
        nxt = jnp.where(has_next, pltpu.roll(gext, text - 1, 0)[mid], 0.0)
        conv = prev * cw_ref[0:1, cols] + gext[mid] * cw_ref[1:2, cols] + nxt * cw_ref[2:3, cols] + cb_ref[:, cols]
        a_scr[:, cols] = (_silu(conv) * val).astype(BF16)
    y_ref[...] = xext[mid] + gt2_ref[...] * _dot(a_scr[...], wd_ref[...])


def _halo_blocks(t, tm, halo):
    per, nblk = tm // halo, t // halo
    return [(tm, lambda i: i), (halo, lambda i: jnp.maximum(i * per - 1, 0)),
            (halo, lambda i: jnp.minimum((i + 1) * per, nblk - 1))]


def _mix_ffn(mixer, tok, mix_in, out_g, w_out, gt1, sh, sc, gt2, g2, w_up, cw, cb, wd, *,
             n_out_batches, n_lat_batches, ctx_len):
    t, d = tok.t, tok.d
    tm = TM_FFN
    row = pl.BlockSpec((None, 1, d), lambda b, i: (b, 0, 0))
    full = lambda a: pl.BlockSpec(a.shape, lambda b, i: (0,) * a.ndim, pipeline_mode=pl.Buffered(1))
    args, specs = [], []
    for rows, blk in _halo_blocks(t, tm, SUBLANES):
        a_, s_ = tok.specs(rows, blk)
        args += a_
        specs += s_
    for a in mix_in:
        halo = SUBLANES * (4 // a.dtype.itemsize)
        for rows, blk in _halo_blocks(t, tm, halo):
            args.append(a)
            specs.append(pl.BlockSpec((None, rows, a.shape[-1]), lambda b, i, blk=blk: (b, blk(i), 0)))
    if mixer == "mlstm":
        args.append(out_g)
        specs.append(full(out_g))
    consts = [w_out, gt1, sh, sc, gt2, g2, w_up, cw, cb, wd]
    args += consts
    specs += [full(w_out), row, row, row, row] + [full(a) for a in consts[5:]]
    return pl.pallas_call(
        functools.partial(_mix_ffn_kernel, mixer=mixer, n_tok_refs=tok.n_refs, seq_len=t, ctx_len=ctx_len,
                          n_lat_batches=n_lat_batches),
        out_shape=jax.ShapeDtypeStruct((n_out_batches, t, d), F32),
        grid=(n_out_batches, t // tm),
        in_specs=specs,
        out_specs=pl.BlockSpec((None, tm, d), lambda b, i: (b, i, 0)),
        scratch_shapes=[pltpu.VMEM((tm, D_FF), BF16)]
        + ([pltpu.VMEM((tm + 2 * SUBLANES, d), BF16)] if mixer == "mlstm" else []),
        compiler_params=_cparams(("parallel", "parallel")),
        name=mixer + "_ffn",
    )(*args)


def _odd_in_kernel(x_ref, xp_ref, xn_ref, sh_ref, sc_ref, g1_ref, w_ref, gb_ref, cw_ref, cb_ref,
                   qk_ref, v_ref, o_ref, g_ref, *, seq_len, ctx_len, n_lat_batches):
    tm = x_ref.shape[0]
    text = tm + 2 * SUBLANES
    mid = slice(SUBLANES, SUBLANES + tm)
    xext = jnp.concatenate([xp_ref[...], x_ref[...], xn_ref[...]], axis=0)
    hparts = _norm_mod_parts(xext, g1_ref[...], sh_ref[...], sc_ref[...])
    hext = jnp.concatenate(hparts, axis=0)
    nqk = qk_ref.shape[1]
    nv = v_ref.shape[1]
    has_prev, has_next = _edge_masks(pl.program_id(1) * tm, tm, pl.program_id(0) < n_lat_batches, seq_len, ctx_len)
    step = 2 * LANES
    for j in range(nqk // step):
        cols = slice(j * step, (j + 1) * step)
        ext = _rdot(hparts if j == 0 else hext, w_ref[:, cols])
        prev = jnp.where(has_prev, pltpu.roll(ext, 1, 0)[mid], 0.0)
        nxt = jnp.where(has_next, pltpu.roll(ext, text - 1, 0)[mid], 0.0)
        conv = prev * cw_ref[0:1, cols] + ext[mid] * cw_ref[1:2, cols] + nxt * cw_ref[2:3, cols] + cb_ref[:, cols]
        scale = MLSTM_QK ** -0.5 if j >= nqk // (2 * step) else 1.0
        qk_ref[:, cols] = (_silu(conv) * scale).astype(BF16)
    v_ref[...] = _dot(hext, w_ref[:, nqk:nqk + nv])[mid].astype(BF16)
    o_ref[...] = _dot(hext, w_ref[:, nqk + nv:nqk + 2 * nv])[mid]
    g_ref[...] = _dot(hext, w_ref[:, nqk + 2 * nv:])[mid] + gb_ref[...]


def _odd_in(X, sh, sc, g1, w, gate_b, cw, cb, *, ctx_len):
    nb, t, d = X.shape
    tm = TM_ODD
    nqk = 2 * MLSTM_HEADS * MLSTM_QK
    nv = MLSTM_HEADS * MLSTM_V
    row = pl.BlockSpec((None, 1, d), lambda b, i: (b, 0, 0))
    full = lambda a: pl.BlockSpec(a.shape, lambda b, i: (0,) * a.ndim)
    tile = lambda wd: pl.BlockSpec((None, tm, wd), lambda b, i: (b, i, 0))
    x_specs = [pl.BlockSpec((None, rows, d), lambda b, i, blk=blk: (b, blk(i), 0))
               for rows, blk in _halo_blocks(t, tm, SUBLANES)]
    return pl.pallas_call(
        functools.partial(_odd_in_kernel, seq_len=t, ctx_len=ctx_len, n_lat_batches=nb - 1),
        out_shape=[jax.ShapeDtypeStruct((nb, t, nqk), BF16), jax.ShapeDtypeStruct((nb, t, nv), BF16),
                   jax.ShapeDtypeStruct((nb, t, nv), F32), jax.ShapeDtypeStruct((nb, t, LANES), F32)],
        grid=(nb, t // tm),
        in_specs=x_specs + [row, row, full(g1), full(w), full(gate_b), full(cw), full(cb)],
        out_specs=[tile(nqk), tile(nv), tile(nv), tile(LANES)],
        compiler_params=_cparams(("parallel", "parallel")),
        name="odd_in",
    )(X, X, X, sh, sc, g1, w, gate_b, cw, cb)


def _log_sigmoid(x):
    return jnp.minimum(x, 0.0) - jnp.log(1.0 + jnp.exp(-jnp.abs(x)))


def _running_max(x, rev):
    n = x.shape[0]
    row = lax.broadcasted_iota(jnp.int32, x.shape, 0)
    k = 1
    while k < n:
        if rev:
            x = jnp.where(row < n - k, jnp.maximum(x, pltpu.roll(x, n - k, 0)), x)
        else:
            x = jnp.where(row >= k, jnp.maximum(x, pltpu.roll(x, k, 0)), x)
        k *= 2
    return x


def _mlstm_kernel(qf_ref, kf_ref, vf_ref, gf_ref, qb_ref, kb_ref, vb_ref, gb_ref, hf_ref, hb_ref, cn_scr, m_scr):
    @pl.when(pl.program_id(1) == 0)
    def _():
        cn_scr[...] = jnp.zeros_like(cn_scr)
        m_scr[...] = jnp.zeros_like(m_scr)

    dirs = (0, 1)
    q_refs, k_refs, v_refs = (qf_ref, qb_ref), (kf_ref, kb_ref), (vf_ref, vb_ref)
    g_refs, h_refs = (gf_ref, gb_ref), (hf_ref, hb_ref)
    L = ML_CHUNK
    n_sub = qf_ref.shape[0] // L
    nh = MLSTM_HEADS
    i_off = (0, 2 * nh)
    last = (L - 1, 0)
    r_i = lax.broadcasted_iota(jnp.int32, (L, L), 0)
    c_i = lax.broadcasted_iota(jnp.int32, (L, L), 1)
    tri = (c_i <= r_i, c_i >= r_i)
    ones = jnp.ones((L, LANES), BF16)
    row = lax.broadcasted_iota(jnp.int32, (LANES, L), 0)
    hc = lambda hd: slice(hd * LANES, (hd + 1) * LANES)
    rows_of = lambda sub: [slice(sub * L, (sub + 1) * L), slice((n_sub - 1 - sub) * L, (n_sub - sub) * L)]

    m_prev = [m_scr[d] for d in dirs]
    gates = []
    for sub in range(n_sub):
        rows = rows_of(sub)
        g = [g_refs[d][rows[d], :] for d in dirs]
        b_f = [jnp.dot(tri[d].astype(F32), _log_sigmoid(g[d]), precision=lax.Precision.HIGHEST,
                       preferred_element_type=F32) for d in dirs]
        b = [pltpu.roll(b_f[d], LANES - nh, 1) for d in dirs]
        u = [g[d] - b[d] for d in dirs]
        cmax = [_running_max(u[d], d == 1) for d in dirs]
        mm = [jnp.maximum(m_prev[d], cmax[d]) for d in dirs]
        em = [jnp.exp(-(b[d] + mm[d])) for d in dirs]
        b_last = [b[d][last[d]:last[d] + 1, :] for d in dirs]
        a = [b_last[d] - b[d] + g[d] for d in dirs]
        m_new = [jnp.maximum(b_last[d] + m_prev[d], jnp.max(a[d], axis=0, keepdims=True)) for d in dirs]
        decay = [jnp.exp(b_last[d] + m_prev[d] - m_new[d]) for d in dirs]
        w = [jnp.exp(a[d] - m_new[d]) for d in dirs]
        u_t = [u[d].T for d in dirs]
        w_t = [w[d].T for d in dirs]
        gates.append((rows, m_prev, mm, em, decay, u_t, w_t))
        m_prev = m_new
    for d in dirs:
        m_scr[d] = m_prev[d]

    for sub in range(n_sub):
        rows, m_prev, mm, em, decay, u_t, w_t = gates[sub]
        for pair in range(nh // 2):
            pc = slice(pair * LANES, (pair + 1) * LANES)
            qp = [q_refs[d][rows[d], pc] for d in dirs]
            kt = [k_refs[d][rows[d], pc].astype(F32).T for d in dirs]
            ch = [(d, 2 * pair + half, i_off[d] + 2 * pair + half, half) for half in range(2) for d in dirs]
            kt_h = [jnp.where((row < MLSTM_QK) if half == 0 else (row >= MLSTM_QK), kt[d], 0.0)
                    for d, hd, col, half in ch]
            v1 = [jnp.concatenate([v_refs[d][rows[d], hc(hd)], ones], axis=1) for d, hd, col, half in ch]
            mm_b = [jnp.broadcast_to(mm[d][:, col:col + 1], (L, LANES)) for d, hd, col, half in ch]
            em_b = [jnp.broadcast_to(em[d][:, col:col + 1], (L, LANES)) for d, hd, col, half in ch]
            dw = [jnp.where(tri[d], jnp.exp(u_t[d][col:col + 1, :] - mm_b[i]), 0.0) for i, (d, hd, col, half) in enumerate(ch)]
            s = [_dot(qp[d], kt_h[i].astype(BF16)) * dw[i] for i, (d, hd, col, half) in enumerate(ch)]
            cn = [cn_scr[d, hd] for d, hd, col, half in ch]
            qc = [_dot(qp[d], cn[i].astype(BF16)) for i, (d, hd, col, half) in enumerate(ch)]
            sv = [_dot(s[i].astype(BF16), v1[i]) for i in range(len(ch))]
            w_inter = [jnp.exp(m_prev[d][:, col:col + 1] - mm_b[i]) for i, (d, hd, col, half) in enumerate(ch)]
            num = [w_inter[i] * qc[i][:, :LANES] + sv[i][:, :LANES] for i in range(len(ch))]
            den = [w_inter[i] * qc[i][:, LANES:] + sv[i][:, LANES:] for i in range(len(ch))]
            for i, (d, hd, col, half) in enumerate(ch):
                h_refs[d][rows[d], hc(hd)] = num[i] * (1.0 / jnp.maximum(jnp.abs(den[i]), em_b[i]))
            ktw = [(kt_h[i] * w_t[d][col:col + 1, :]).astype(BF16) for i, (d, hd, col, half) in enumerate(ch)]
            upd = [_dot(ktw[i], v1[i]) for i in range(len(ch))]
            for i, (d, hd, col, half) in enumerate(ch):
                cn_scr[d, hd] = decay[d][:, col:col + 1] * cn[i] + upd[i]


def _mlstm(QK, V, G, *, ctx_len):
    nb, t, nqk = QK.shape
    nbl = nb - 1
    L = ML_CHUNK * ML_CHUNKS_PER_STEP
    assert ctx_len % L == 0
    ncc, nlc = ctx_len // L, t // L
    nq = nqk // 2
    nv = V.shape[-1]

    def spec(width, col, rev):
        def index(b, c):
            cc, lc = (ncc - 1 - c, nlc - 1 - (c - ncc)) if rev else (c, c - ncc)
            is_ctx = c < ncc
            return jnp.where(is_ctx, nbl, b), jnp.where(is_ctx, b * ncc + cc, lc), col
        return pl.BlockSpec((None, L, width), index)

    ins = lambda rev: [spec(nq, 0, rev), spec(nq, 1, rev), spec(nv, 0, rev), spec(LANES, 0, rev)]
    out = jax.ShapeDtypeStruct((nb, t, nv), F32)
    return pl.pallas_call(
        _mlstm_kernel,
        out_shape=[out, out],
        grid=(nbl, ncc + nlc),
        in_specs=ins(False) + ins(True),
        out_specs=[spec(nv, 0, False), spec(nv, 0, True)],
        scratch_shapes=[pltpu.VMEM((2, MLSTM_HEADS, LANES, MLSTM_V + LANES), F32),
                        pltpu.VMEM((2, 1, LANES), F32)],
        compiler_params=_cparams(("parallel", "arbitrary")),
        name="mlstm",
    )(QK, QK, V, G, QK, QK, V, G)


def kernel(x, c, ctx, c_ctx, ada_w, ada_b, norm1_g, norm2_g, ffn_w_up, ffn_conv_w, ffn_conv_b, ffn_w_down,
           att_w_in, mla_qa_g, mla_w_qb, mla_kva_g, mla_w_kvb, mla_q_g, mla_k_g, gqa_q_g, gqa_k_g, att_w_out,
           ml_w_in, ml_conv_w, ml_conv_b, ml_gate_b, ml_out_g, ml_w_out):
    B, T, D = x.shape
    ctx_len = ctx.shape[1]
    depth = ada_w.shape[0]
    assert B * ctx_len == T, "context sequences must tile one latent-length row of the token array"
    NB = B + 1

    tok = _Tokens(x, ctx.reshape(1, T, D))

    rows = -(-NB // SUBLANES) * SUBLANES
    cvec = jnp.zeros((rows, D), F32).at[:B].set(c).at[B].set(c_ctx)
    mod = _adaln(cvec, ada_w, ada_b)
    mod = mod.reshape(depth, rows, 6, 1, D).transpose(0, 2, 1, 3, 4)

    rope_m = _rope_tables(T, MLA_ROPE, MLA_NOPE)
    rope_g = _rope_tables(T, GQA_DIM, 0, GQA_DIM)

    for layer in range(depth):
        last = layer == depth - 1
        j = layer // 2
        sh1, sc1, gt1, sh2, sc2, gt2 = (mod[layer, i] for i in range(6))
        g1 = norm1_g[layer][None, :]
        g2 = norm2_g[layer][None, :]
        if layer % 2 == 0:
            ew = _even_weights(att_w_in[j], mla_w_qb[j], mla_w_kvb[j], mla_q_g[j], mla_k_g[j],
                               gqa_q_g[j], gqa_k_g[j])
            qm, km, vm, qg, kg, vg = _even_in(tok, sh1, sc1, g1, ew, mla_qa_g[j][None, :], mla_kva_g[j][None, :],
                                              rope_m, rope_g)
            oa = _attention(qm, km, vm, n_heads=MLA_HEADS, q_pack=1, k_share=1, k_variants=1, v_share=2, ctx_len=ctx_len)
            rep = GQA_HEADS // GQA_KV_HEADS
            ob = _attention(qg, kg, vg, n_heads=GQA_HEADS, q_pack=2, k_share=rep, k_variants=2, v_share=rep,
                            ctx_len=ctx_len)
            mixer, mix_in, out_g, w_out = "attn", (oa, ob), None, att_w_out[j]
        else:
            assert tok.ctx is None, "an odd layer reads the assembled token array"
            gate_b = _pad_cols(ml_gate_b[j][None, :], LANES)
            w_in = jnp.pad(ml_w_in[j].astype(BF16), ((0, 0), (0, LANES - 4 * MLSTM_HEADS)))
            qk, v, o, gates = _odd_in(tok.lat, sh1, sc1, g1, w_in, gate_b, ml_conv_w[j], ml_conv_b[j][None, :],
                                      ctx_len=ctx_len)
            hf, hb = _mlstm(qk, v, gates, ctx_len=ctx_len)
            mixer, mix_in, out_g, w_out = "mlstm", (hf, hb, o), ml_out_g[j].reshape(1, -1), ml_w_out[j]
        X = _mix_ffn(mixer, tok, mix_in, out_g, w_out.astype(BF16), gt1, sh2, sc2, gt2, g2,
                     ffn_w_up[layer].astype(BF16), ffn_conv_w[layer], ffn_conv_b[layer][None, :],
                     ffn_w_down[layer].astype(BF16),
                     n_out_batches=B if last else NB, n_lat_batches=B, ctx_len=ctx_len)
        tok = _Tokens(X)
    return X[:B]
```

```python
import functools

import numpy as np
import jax
import jax.numpy as jnp
from jax import lax
from jax.experimental import pallas as pl
from jax.experimental.pallas import tpu as pltpu

F32 = jnp.float32
BF16 = jnp.bfloat16

EPS = 1e-6
ROPE_THETA = 10000.0
GRID_W = 64
LOG2E = 1.4426950408889634

MLA_HEADS = 8
MLA_Q_RANK = 384
MLA_KV_RANK = 256
MLA_NOPE = 64
MLA_ROPE = 32
MLA_V = 64
MLA_QK = MLA_NOPE + MLA_ROPE
GQA_HEADS = 8
GQA_KV_HEADS = 2
GQA_DIM = 64
MLSTM_HEADS = 8
MLSTM_QK = 64
MLSTM_V = 128
D_FF = 2816

LANES = 128
SUBLANES = 8
VMEM_LIMIT = 56 * 1024 * 1024

TM_IN = 512
TM_ODD = 512
TM_FFN = 512
FF_CHUNK = 256
TQ = 512
KV_CHUNK = 512
ML_CHUNK = 128
ML_CHUNKS_PER_STEP = 2


def _cparams(sem):
    return pltpu.CompilerParams(dimension_semantics=sem, vmem_limit_bytes=VMEM_LIMIT)


def _dot(a, b):
    return jnp.dot(a, b, preferred_element_type=F32)


def _dot_nt(a, b):
    return lax.dot_general(a, b, (((1,), (1,)), ((), ())), preferred_element_type=F32)


def _rms(x, gain_row, inv_n):
    ss = jnp.sum(x * x, axis=-1, keepdims=True) * inv_n
    return x * lax.rsqrt(ss + EPS) * gain_row


def _norm_mod(x, g_row, sh_row, sc_row):
    ms = jnp.mean(x * x, axis=-1, keepdims=True)
    return x * lax.rsqrt(ms + EPS) * (g_row * (1.0 + sc_row)) + sh_row


def _n_row_parts(rows):
    tiles = rows // (2 * SUBLANES)
    assert tiles * 2 * SUBLANES == rows
    return next(n for n in (3, 5, 2, 1) if tiles % n == 0)


def _norm_mod_parts(x, g_row, sh_row, sc_row):
    n_parts = _n_row_parts(x.shape[0])
    rp = x.shape[0] // n_parts
    return [_norm_mod(x[k * rp:(k + 1) * rp], g_row, sh_row, sc_row).astype(BF16) for k in range(n_parts)]


def _rdot(lhs, w):
    if isinstance(lhs, (list, tuple)):
        return jnp.concatenate([_dot(p, w) for p in lhs], axis=0)
    return _dot(lhs, w)


def _silu(x):
    return x * (1.0 / (1.0 + jnp.exp(-x)))


def _edge_masks(row0, n_rows, is_latent, seq_len, ctx_len):
    assert seq_len & (seq_len - 1) == 0 and ctx_len & (ctx_len - 1) == 0, "sequence lengths must be powers of two"
    last = jnp.where(is_latent, seq_len - 1, ctx_len - 1)
    pos = (row0 + lax.broadcasted_iota(jnp.int32, (n_rows, 1), 0)) & last
    return pos != 0, pos != last


def _lockstep(chains):
    chains = list(chains)
    while chains:
        for c in list(chains):
            try:
                next(c)
            except StopIteration:
                chains.remove(c)


class _Tokens:
    def __init__(self, lat, ctx=None):
        self.lat, self.ctx = lat, ctx
        self.nb = lat.shape[0] + (0 if ctx is None else 1)
        self.t, self.d = lat.shape[1], lat.shape[2]
        self.n_refs = 1 if ctx is None else 2

    def specs(self, rows, blk):
        if self.ctx is None:
            return [self.lat], [pl.BlockSpec((None, rows, self.d), lambda b, i: (b, blk(i), 0))]
        nbl = self.nb - 1
        lat = pl.BlockSpec((None, rows, self.d), lambda b, i: (jnp.minimum(b, nbl - 1), jnp.where(b < nbl, blk(i), 0), 0))
        ctx = pl.BlockSpec((None, rows, self.d), lambda b, i: (0, jnp.where(b < nbl, 0, blk(i)), 0))
        return [self.lat, self.ctx], [lat, ctx]


def _tok(refs, is_latent):
    if len(refs) == 1:
        return refs[0][...]
    return jnp.where(is_latent, refs[0][...], refs[1][...])


def _adaln_kernel(c_ref, w_ref, b_ref, o_ref):
    c = _silu(c_ref[...]).astype(BF16)
    o_ref[...] = _dot(c, w_ref[...].astype(BF16)) + b_ref[...]


def _adaln(cvec, ada_w, ada_b):
    depth, d, n = ada_w.shape
    rows = cvec.shape[0]
    tn = 1536
    return pl.pallas_call(
        _adaln_kernel,
        out_shape=jax.ShapeDtypeStruct((depth, rows, n), F32),
        grid=(depth, n // tn),
        in_specs=[
            pl.BlockSpec((rows, d), lambda l, j: (0, 0)),
            pl.BlockSpec((None, d, tn), lambda l, j: (l, 0, j)),
            pl.BlockSpec((None, 1, tn), lambda l, j: (l, 0, j)),
        ],
        out_specs=pl.BlockSpec((None, rows, tn), lambda l, j: (l, 0, j)),
        compiler_params=_cparams(("arbitrary", "arbitrary")),
        name="adaln",
    )(cvec, ada_w, ada_b.reshape(depth, 1, n))


def _rope(y, cos, sa, sb, shift):
    return y * cos + pltpu.roll(y, LANES - shift, 1) * sa + pltpu.roll(y, shift, 1) * sb


def _group_sumsq(y, width=LANES):
    n = y.shape[1]
    r = lax.broadcasted_iota(jnp.int32, (n, n), 0) // width
    c = lax.broadcasted_iota(jnp.int32, (n, n), 1) // width
    return _dot((y * y).astype(BF16), jnp.where(r == c, 1.0, 0.0).astype(BF16))


def _rope_pair(y, tabs, shift):
    cos, sa, sb = tabs
    return jnp.concatenate([_rope(y[:, :LANES], cos, sa, sb, shift), _rope(y[:, LANES:], cos, sa, sb, shift)], axis=1)


def _even_in_kernel(*refs, n_tok_refs, n_lat_batches):
    x_refs, refs = refs[:n_tok_refs], refs[n_tok_refs:]
    (sh_ref, sc_ref, g1_ref, win_ref, gqa_ref, wqb_ref, gkva_ref, wk_ref, wv_ref,
     gmq_ref, gmk_ref, ggq_ref, ggk_ref, rm_ref, rg_ref,
     qm_ref, km_ref, vm_ref, qg_ref, kg_ref, vg_ref) = refs
    x = _tok(x_refs, pl.program_id(0) < n_lat_batches)
    hparts = _norm_mod_parts(x, g1_ref[...], sh_ref[...], sc_ref[...])
    h = jnp.concatenate(hparts, axis=0)
    rope_m = (rm_ref[0], rm_ref[1], rm_ref[2])
    rope_g = (rg_ref[0], rg_ref[1], rg_ref[2])
    pw = 2 * LANES
    o_ckv = MLA_Q_RANK
    o_kr = o_ckv + MLA_KV_RANK
    o_gq = o_kr + LANES
    o_gk = o_gq + GQA_HEADS * GQA_DIM
    o_gv = o_gk + GQA_KV_HEADS * LANES
    two = lambda a: jnp.concatenate([a, a], axis=1)

    low = _rdot(hparts, win_ref[:, 0:o_gq])
    cqn = _rms(low[:, 0:o_ckv], gqa_ref[...], 1.0 / MLA_Q_RANK).astype(BF16)
    ckvn = _rms(low[:, o_ckv:o_kr], gkva_ref[...], 1.0 / MLA_KV_RANK).astype(BF16)
    krp = low[:, o_kr:o_gq]
    gmk = gmk_ref[...]
    kr_rot = two(_rope(krp * gmk[:, :LANES], *rope_m, MLA_ROPE // 4))
    kr_ss = _group_sumsq(two(krp))
    vm_ref[...] = _dot(ckvn, wv_ref[...]).astype(BF16)
    vg_ref[...] = _dot(h, win_ref[:, o_gv:o_gv + GQA_KV_HEADS * LANES]).astype(BF16)

    def mla_query(p):
        blk = _dot(cqn, wqb_ref[:, p * pw:(p + 1) * pw])
        yield
        ss = _group_sumsq(blk)
        yield
        y = blk * lax.rsqrt(ss * (1.0 / MLA_QK) + EPS) * gmq_ref[...]
        yield
        qm_ref[:, p * pw:(p + 1) * pw] = _rope_pair(y, rope_m, MLA_ROPE // 4).astype(BF16)

    def mla_key(p):
        blk = _dot(ckvn, wk_ref[:, p * pw:(p + 1) * pw])
        yield
        ss = _group_sumsq(blk)
        yield
        r = lax.rsqrt((ss + kr_ss) * (1.0 / MLA_QK) + EPS)
        yield
        km_ref[:, p * pw:(p + 1) * pw] = (r * (blk * gmk + kr_rot)).astype(BF16)

    def gqa_query(p):
        blk = _dot(h, win_ref[:, o_gq + p * pw:o_gq + (p + 1) * pw])
        yield
        ss = _group_sumsq(blk, GQA_DIM)
        yield
        y = blk * lax.rsqrt(ss * (1.0 / GQA_DIM) + EPS) * ggq_ref[...]
        yield
        qg_ref[:, p * pw:(p + 1) * pw] = _rope_pair(y, rope_g, GQA_DIM // 4).astype(BF16)

    def gqa_key(p):
        blk = _dot(h, win_ref[:, o_gk + p * pw:o_gk + (p + 1) * pw])
        yield
        ss = _group_sumsq(blk)
        yield
        y = blk * lax.rsqrt(ss * (1.0 / GQA_DIM) + EPS) * ggk_ref[...]
        yield
        k2 = _rope_pair(y, rope_g, GQA_DIM // 4)
        for i in range(2):
            lo = k2[:, i * LANES:(i + 1) * LANES]
            c0 = (2 * p + i) * pw
            kg_ref[:, c0:c0 + LANES] = lo.astype(BF16)
            kg_ref[:, c0 + LANES:c0 + pw] = pltpu.roll(lo, GQA_DIM, 1).astype(BF16)

    for p in range(MLA_HEADS // 2):
        chains = [mla_query(p), mla_key(p)]
        if p < GQA_HEADS * GQA_DIM // pw:
            chains.append(gqa_query(p))
        if p < GQA_KV_HEADS // 2:
            chains.append(gqa_key(p))
        _lockstep(chains)


def _pad_cols(w, width):
    return jnp.pad(w, ((0, 0), (0, width - w.shape[1])))


def _even_weights(w_in, w_qb, w_kvb, g_mq, g_mk, g_gq, g_gk):
    w_in, w_qb, w_kvb = w_in.astype(BF16), w_qb.astype(BF16), w_kvb.astype(BF16)
    d = w_in.shape[0]
    s0 = MLA_Q_RANK
    s1 = s0 + MLA_KV_RANK
    s2 = s1 + MLA_ROPE
    s3 = s2 + GQA_HEADS * GQA_DIM
    s4 = s3 + GQA_KV_HEADS * GQA_DIM
    cq, ckv, kr = w_in[:, :s0], w_in[:, s0:s1], w_in[:, s1:s2]
    gq, gk, gv = w_in[:, s2:s3], w_in[:, s3:s4], w_in[:, s4:]
    kr_p = jnp.pad(kr, ((0, 0), (MLA_NOPE, LANES - MLA_QK)))
    gk_p = jnp.pad(gk.reshape(d, GQA_KV_HEADS, GQA_DIM), ((0, 0), (0, 0), (0, LANES - GQA_DIM))).reshape(d, -1)
    gv3 = gv.reshape(d, GQA_KV_HEADS, GQA_DIM)
    gv_p = jnp.concatenate([gv3, gv3], axis=-1).reshape(d, -1)
    win = jnp.concatenate([cq, ckv, kr_p, gq, gk_p, gv_p], axis=1)
    wqb = jnp.pad(w_qb.reshape(MLA_Q_RANK, MLA_HEADS, MLA_QK),
                  ((0, 0), (0, 0), (0, LANES - MLA_QK))).reshape(MLA_Q_RANK, -1)
    kv3 = w_kvb.reshape(MLA_KV_RANK, MLA_HEADS, MLA_NOPE + MLA_V)
    wk = jnp.pad(kv3[:, :, :MLA_NOPE], ((0, 0), (0, 0), (0, LANES - MLA_NOPE))).reshape(MLA_KV_RANK, -1)
    wv = kv3[:, :, MLA_NOPE:].reshape(MLA_KV_RANK, -1)
    pair = lambda g: jnp.tile(_pad_cols(g[None, :], LANES), (1, 2))
    gmq = pair(g_mq * (MLA_QK ** -0.5 * LOG2E))
    gmk = pair(g_mk)
    ggq = jnp.tile((g_gq * (GQA_DIM ** -0.5 * LOG2E))[None, :], (1, 2 * LANES // GQA_DIM))
    ggk = pair(g_gk)
    return win, wqb, wk, wv, gmq, gmk, ggq, ggk


def _rope_tables(n_tokens, rot_dim, offset, period=LANES):
    rows = n_tokens // GRID_W
    row = np.repeat(np.arange(rows), GRID_W).astype(np.float32)
    col = np.tile(np.arange(GRID_W), rows).astype(np.float32)
    n_freq = rot_dim // 4
    inv = (np.float32(ROPE_THETA) ** (-np.arange(n_freq, dtype=np.float32) / np.float32(n_freq))).astype(np.float32)
    a_r, a_c = row[:, None] * inv, col[:, None] * inv
    ang = np.concatenate([a_r, a_r, a_c, a_c], axis=-1)
    cos, sin = np.cos(ang), np.sin(ang)
    quarter = (np.arange(rot_dim) // n_freq) % 2
    sa = np.where(quarter == 0, -sin, 0.0)
    sb = np.where(quarter == 1, sin, 0.0)
    pad = ((0, 0), (offset, period - offset - rot_dim))
    rep = lambda a: np.tile(a, (1, LANES // period))
    lat = np.stack([rep(np.pad(cos, pad, constant_values=1.0)), rep(np.pad(sa, pad)), rep(np.pad(sb, pad))])
    ident = np.stack([np.ones((n_tokens, LANES)), np.zeros((n_tokens, LANES)), np.zeros((n_tokens, LANES))])
    return jnp.asarray(np.stack([lat, ident]), F32)


def _even_in(tok, sh, sc, g1, ew, gqa, gkva, rope_m, rope_g):
    nb, t, d = tok.nb, tok.t, tok.d
    win, wqb, wk, wv, gmq, gmk, ggq, ggk = ew
    tm = TM_IN
    full = lambda a: pl.BlockSpec(a.shape, lambda b, i: (0,) * a.ndim)
    row = pl.BlockSpec((None, 1, d), lambda b, i: (b, 0, 0))
    rope_spec = pl.BlockSpec((None, 3, tm, LANES), lambda b, i: (jnp.where(b == nb - 1, 1, 0), 0, i, 0))
    x_args, x_specs = tok.specs(tm, lambda i: i)

    def out(width):
        return jax.ShapeDtypeStruct((nb, t, width), BF16), pl.BlockSpec((None, tm, width), lambda b, i: (b, i, 0))

    outs = [out(MLA_HEADS * LANES), out(MLA_HEADS * LANES), out(MLA_HEADS * MLA_V),
            out(GQA_HEADS * GQA_DIM), out(GQA_KV_HEADS * 2 * LANES), out(GQA_KV_HEADS * LANES)]
    return pl.pallas_call(
        functools.partial(_even_in_kernel, n_tok_refs=tok.n_refs, n_lat_batches=nb - 1),
        out_shape=[o[0] for o in outs],
        grid=(nb, t // tm),
        in_specs=x_specs + [row, row, full(g1), full(win), full(gqa),
                            full(wqb), full(gkva), full(wk), full(wv), full(gmq), full(gmk), full(ggq), full(ggk),
                            rope_spec, rope_spec],
        out_specs=[o[1] for o in outs],
        compiler_params=_cparams(("parallel", "parallel")),
        name="even_in",
    )(*x_args, sh, sc, g1, win, gqa, wqb, gkva, wk, wv, gmq, gmk, ggq, ggk, rope_m, rope_g)


def _attn_kernel(q_ref, kl_ref, kc_ref, vl_ref, vc_ref, o_ref, s_scr, p_scr, *,
                 n_heads, q_pack, k_share, k_variants, v_share, ctx_len, n_lat_batches):
    tq = q_ref.shape[0]

    def run(q0, nq, parts):
        lane = lax.broadcasted_iota(jnp.int32, (nq, LANES), 1)
        chunks, off = [], 0
        for k_ref, _, base, n in parts:
            size = min(KV_CHUNK, n)
            for c in range(n // size):
                chunks.append((k_ref, base + c * size, size, off + c * size))
            off += n
        mx, m, lsum = {}, {}, {}
        outs = [None] * n_heads

        def scores(hd, ci):
            k_ref, r0, size, c0 = chunks[ci]
            qb = hd // q_pack
            kb = (hd // k_share) * k_variants + hd % k_variants
            s = _dot_nt(q_ref[q0:q0 + nq, qb * LANES:(qb + 1) * LANES],
                        k_ref[pl.ds(r0, size), kb * LANES:(kb + 1) * LANES])
            s_scr[hd % 2, 0:nq, c0:c0 + size] = s
            for j in range(size // LANES):
                blk = s[:, j * LANES:(j + 1) * LANES]
                mx[hd] = blk if hd not in mx else jnp.maximum(mx[hd], blk)

        def values(hd, ci):
            _, _, size, c0 = chunks[ci]
            p = jnp.exp2(s_scr[hd % 2, 0:nq, c0:c0 + size] - m[hd])
            for j in range(size // LANES):
                blk = p[:, j * LANES:(j + 1) * LANES]
                lsum[hd] = blk if hd not in lsum else lsum[hd] + blk
            p_scr[hd % 2, 0:nq, c0:c0 + size] = p.astype(BF16)

        def product(hd):
            vb = hd // v_share
            out, off = None, 0
            for _, v_ref, base, n in parts:
                part = _dot(p_scr[hd % 2, 0:nq, off:off + n], v_ref[pl.ds(base, n), vb * LANES:(vb + 1) * LANES])
                out = part if out is None else out + part
                off += n
            return out

        for j in range(n_heads + 1):
            for ci in range(len(chunks)):
                if j < n_heads:
                    scores(j, ci)
                if j >= 1:
                    values(j - 1, ci)
            if j < n_heads:
                m[j] = jnp.max(mx.pop(j), axis=-1, keepdims=True)
            if j >= 1:
                outs[j - 1] = product(j - 1) * (1.0 / jnp.sum(lsum.pop(j - 1), axis=-1, keepdims=True))
        for j in range(n_heads // 2):
            pair = jnp.where(lane < LANES // 2, outs[2 * j], outs[2 * j + 1])
            o_ref[q0:q0 + nq, j * LANES:(j + 1) * LANES] = pair.astype(o_ref.dtype)

    b = pl.program_id(0)
    is_lat = b < n_lat_batches
    per_blk = kc_ref.shape[0] // ctx_len

    @pl.when(is_lat)
    def _():
        own = pl.multiple_of((b % per_blk) * ctx_len, ctx_len)
        run(0, tq, [(kc_ref, vc_ref, own, ctx_len), (kl_ref, vl_ref, 0, kl_ref.shape[0])])

    @pl.when(jnp.logical_not(is_lat))
    def _():
        for sub in range(tq // ctx_len):
            run(sub * ctx_len, ctx_len, [(kc_ref, vc_ref, sub * ctx_len, ctx_len)])


def _attention(Q, K, V, *, n_heads, q_pack, k_share, k_variants, v_share, ctx_len):
    nb, t, qw = Q.shape
    nbl = nb - 1
    assert TQ % ctx_len == 0 and nbl * ctx_len == t
    per_blk = TQ // ctx_len
    kw, vw, ow = K.shape[-1], V.shape[-1], n_heads * LANES // 2
    lat_b = lambda b: jnp.minimum(b, nbl - 1)
    ctx_blk = lambda b, i: jnp.where(b < nbl, b // per_blk, i)
    return pl.pallas_call(
        functools.partial(_attn_kernel, n_heads=n_heads, q_pack=q_pack, k_share=k_share, k_variants=k_variants,
                          v_share=v_share, ctx_len=ctx_len, n_lat_batches=nbl),
        out_shape=jax.ShapeDtypeStruct((nb, t, ow), BF16),
        grid=(nb, t // TQ),
        in_specs=[
            pl.BlockSpec((None, TQ, qw), lambda b, i: (b, i, 0)),
            pl.BlockSpec((None, t, kw), lambda b, i: (lat_b(b), 0, 0)),
            pl.BlockSpec((None, TQ, kw), lambda b, i: (nbl, ctx_blk(b, i), 0)),
            pl.BlockSpec((None, t, vw), lambda b, i: (lat_b(b), 0, 0)),
            pl.BlockSpec((None, TQ, vw), lambda b, i: (nbl, ctx_blk(b, i), 0)),
        ],
        out_specs=pl.BlockSpec((None, TQ, ow), lambda b, i: (b, i, 0)),
        scratch_shapes=[pltpu.VMEM((2, TQ, t + ctx_len), F32), pltpu.VMEM((2, TQ, t + ctx_len), BF16)],
        compiler_params=_cparams(("parallel", "arbitrary")),
        name="attn_q%d_k%d" % (q_pack, k_share),
    )(Q, K, K, V, V)


def _ext_rows(main, prev, nxt):
    hp = prev.shape[0]
    return jnp.concatenate([prev.astype(F32)[hp - SUBLANES:hp], main.astype(F32), nxt.astype(F32)[0:SUBLANES]], axis=0)


def _mix_ffn_kernel(*refs, mixer, n_tok_refs, seq_len, ctx_len, n_lat_batches):
    is_latent = pl.program_id(0) < n_lat_batches
    x_parts = [_tok(refs[k * n_tok_refs:(k + 1) * n_tok_refs], is_latent) for k in range(3)]
    refs = refs[3 * n_tok_refs:]
    ext = lambda trio: _ext_rows(*(r[...] for r in trio))
    if mixer == "attn":
        oa_refs, ob_refs, refs = refs[:3], refs[3:6], refs[6:]
    else:
        hf_refs, hb_refs, o_refs, og_ref, refs = refs[:3], refs[3:6], refs[6:9], refs[9], refs[10:]
    (wo_ref, gt1_ref, sh_ref, sc_ref, gt2_ref, g2_ref, wu_ref, cw_ref, cb_ref, wd_ref, y_ref, a_scr) = refs[:12]
    tm = y_ref.shape[0]
    text = tm + 2 * SUBLANES
    mid = slice(SUBLANES, SUBLANES + tm)

    n_parts = _n_row_parts(text)
    rp = text // n_parts
    x_raw = _ext_rows(*x_parts)
    if mixer == "attn":
        oa = ext(oa_refs).astype(BF16)
        ob = ext(ob_refs).astype(BF16)
        na = oa.shape[1]
    else:
        m_scr = refs[12]
        hf, hb, og = ext(hf_refs), ext(hb_refs), ext(o_refs)
    xparts, hparts = [], []
    for k in range(n_parts):
        rows = slice(k * rp, (k + 1) * rp)
        if mixer == "attn":
            mix = _dot(oa[rows], wo_ref[0:na, :]) + _dot(ob[rows], wo_ref[na:, :])
        else:
            for hd in range(MLSTM_HEADS):
                cols = slice(hd * LANES, (hd + 1) * LANES)
                hn = _rms(hf[rows, cols] + hb[rows, cols], og_ref[:, cols], 1.0 / MLSTM_V)
                m_scr[rows, cols] = (hn * (1.0 / (1.0 + jnp.exp(-og[rows, cols])))).astype(BF16)
            mix = _dot(m_scr[rows, :], wo_ref[...])
        xk = x_raw[rows] + gt1_ref[...] * mix
        xparts.append(xk)
        hparts.append(_norm_mod(xk, g2_ref[...], sh_ref[...], sc_ref[...]).astype(BF16))
    xext = jnp.concatenate(xparts, axis=0)
    hext = jnp.concatenate(hparts, axis=0)
    has_prev, has_next = _edge_masks(pl.program_id(1) * tm, tm, is_latent, seq_len, ctx_len)
    for j in range(D_FF // FF_CHUNK):
        cols = slice(j * FF_CHUNK, (j + 1) * FF_CHUNK)
        lhs = hparts if j == 0 else hext
        gext = _rdot(lhs, wu_ref[:, cols])
        val = _rdot(lhs, wu_ref[:, D_FF + j * FF_CHUNK:D_FF + (j + 1) * FF_CHUNK])[mid]
        prev = jnp.where(has_prev, pltpu.roll(gext, 1, 0)[mid], 0.0)
        nxt = jnp.where(has_next, pltpu.roll(gext, text - 1, 0)[mid], 0.0)
        conv = prev * cw_ref[0:1, cols] + gext[mid] * cw_ref[1:2, cols] + nxt * cw_ref[2:3, cols] + cb_ref[:, cols]
        a_scr[:, cols] = (_silu(conv) * val).astype(BF16)
    y_ref[...] = xext[mid] + gt2_ref[...] * _dot(a_scr[...], wd_ref[...])


def _halo_blocks(t, tm, halo):
    per, nblk = tm // halo, t // halo
    return [(tm, lambda i: i), (halo, lambda i: jnp.maximum(i * per - 1, 0)),
            (halo, lambda i: jnp.minimum((i + 1) * per, nblk - 1))]


def _mix_ffn(mixer, tok, mix_in, out_g, w_out, gt1, sh, sc, gt2, g2, w_up, cw, cb, wd, *,
             layer, n_out_batches, n_lat_batches, ctx_len):
    t, d = tok.t, tok.d
    tm = TM_FFN
    row = pl.BlockSpec((None, 1, d), lambda b, i: (b, 0, 0))
    full = lambda a: pl.BlockSpec(a.shape, lambda b, i: (0,) * a.ndim, pipeline_mode=pl.Buffered(1))
    args, specs = [], []
    for rows, blk in _halo_blocks(t, tm, SUBLANES):
        a_, s_ = tok.specs(rows, blk)
        args += a_
        specs += s_
    for a in mix_in:
        halo = SUBLANES * (4 // a.dtype.itemsize)
        for rows, blk in _halo_blocks(t, tm, halo):
            args.append(a)
            specs.append(pl.BlockSpec((None, rows, a.shape[-1]), lambda b, i, blk=blk: (b, blk(i), 0)))
    if mixer == "mlstm":
        args.append(out_g)
        specs.append(full(out_g))
    stacked = lambda a: pl.BlockSpec((None,) + a.shape[1:], lambda b, i: (layer,) + (0,) * (a.ndim - 1),
                                     pipeline_mode=pl.Buffered(1))
    args += [w_out, gt1, sh, sc, gt2, g2, w_up, cw, cb, wd]
    specs += [full(w_out), row, row, row, row, full(g2), stacked(w_up), stacked(cw), stacked(cb), stacked(wd)]
    return pl.pallas_call(
        functools.partial(_mix_ffn_kernel, mixer=mixer, n_tok_refs=tok.n_refs, seq_len=t, ctx_len=ctx_len,
                          n_lat_batches=n_lat_batches),
        out_shape=jax.ShapeDtypeStruct((n_out_batches, t, d), F32),
        grid=(n_out_batches, t // tm),
        in_specs=specs,
        out_specs=pl.BlockSpec((None, tm, d), lambda b, i: (b, i, 0)),
        scratch_shapes=[pltpu.VMEM((tm, D_FF), BF16)]
        + ([pltpu.VMEM((tm + 2 * SUBLANES, d), BF16)] if mixer == "mlstm" else []),
        compiler_params=_cparams(("parallel", "parallel")),
        name=mixer + "_ffn",
    )(*args)


def _odd_in_kernel(x_ref, xp_ref, xn_ref, sh_ref, sc_ref, g1_ref, w_ref, gb_ref, cw_ref, cb_ref,
                   qk_ref, v_ref, o_ref, g_ref, *, seq_len, ctx_len, n_lat_batches):
    tm = x_ref.shape[0]
    text = tm + 2 * SUBLANES
    mid = slice(SUBLANES, SUBLANES + tm)
    xext = jnp.concatenate([xp_ref[...], x_ref[...], xn_ref[...]], axis=0)
    hparts = _norm_mod_parts(xext, g1_ref[...], sh_ref[...], sc_ref[...])
    hext = jnp.concatenate(hparts, axis=0)
    nqk = qk_ref.shape[1]
    nv = v_ref.shape[1]
    has_prev, has_next = _edge_masks(pl.program_id(1) * tm, tm, pl.program_id(0) < n_lat_batches, seq_len, ctx_len)
    step = 2 * LANES
    for j in range(nqk // step):
        cols = slice(j * step, (j + 1) * step)
        ext = _rdot(hparts if j == 0 else hext, w_ref[:, cols])
        prev = jnp.where(has_prev, pltpu.roll(ext, 1, 0)[mid], 0.0)
        nxt = jnp.where(has_next, pltpu.roll(ext, text - 1, 0)[mid], 0.0)
        conv = prev * cw_ref[0:1, cols] + ext[mid] * cw_ref[1:2, cols] + nxt * cw_ref[2:3, cols] + cb_ref[:, cols]
        scale = MLSTM_QK ** -0.5 if j >= nqk // (2 * step) else 1.0
        qk_ref[:, cols] = (_silu(conv) * scale).astype(BF16)
    v_ref[...] = _dot(hext, w_ref[:, nqk:nqk + nv])[mid].astype(BF16)
    o_ref[...] = _dot(hext, w_ref[:, nqk + nv:nqk + 2 * nv])[mid]
    g_ref[...] = _dot(hext, w_ref[:, nqk + 2 * nv:])[mid] + gb_ref[...]


def _odd_in(X, sh, sc, g1, w, gate_b, cw, cb, *, ctx_len):
    nb, t, d = X.shape
    tm = TM_ODD
    nqk = 2 * MLSTM_HEADS * MLSTM_QK
    nv = MLSTM_HEADS * MLSTM_V
    row = pl.BlockSpec((None, 1, d), lambda b, i: (b, 0, 0))
    full = lambda a: pl.BlockSpec(a.shape, lambda b, i: (0,) * a.ndim)
    tile = lambda wd: pl.BlockSpec((None, tm, wd), lambda b, i: (b, i, 0))
    x_specs = [pl.BlockSpec((None, rows, d), lambda b, i, blk=blk: (b, blk(i), 0))
               for rows, blk in _halo_blocks(t, tm, SUBLANES)]
    return pl.pallas_call(
        functools.partial(_odd_in_kernel, seq_len=t, ctx_len=ctx_len, n_lat_batches=nb - 1),
        out_shape=[jax.ShapeDtypeStruct((nb, t, nqk), BF16), jax.ShapeDtypeStruct((nb, t, nv), BF16),
                   jax.ShapeDtypeStruct((nb, t, nv), F32), jax.ShapeDtypeStruct((nb, t, LANES), F32)],
        grid=(nb, t // tm),
        in_specs=x_specs + [row, row, full(g1), full(w), full(gate_b), full(cw), full(cb)],
        out_specs=[tile(nqk), tile(nv), tile(nv), tile(LANES)],
        compiler_params=_cparams(("parallel", "parallel")),
        name="odd_in",
    )(X, X, X, sh, sc, g1, w, gate_b, cw, cb)


def _log_sigmoid(x):
    return jnp.minimum(x, 0.0) - jnp.log(1.0 + jnp.exp(-jnp.abs(x)))


def _running_max(x, rev):
    n = x.shape[0]
    row = lax.broadcasted_iota(jnp.int32, x.shape, 0)
    k = 1
    while k < n:
        if rev:
            x = jnp.where(row < n - k, jnp.maximum(x, pltpu.roll(x, n - k, 0)), x)
        else:
            x = jnp.where(row >= k, jnp.maximum(x, pltpu.roll(x, k, 0)), x)
        k *= 2
    return x


def _mlstm_kernel(qf_ref, kf_ref, vf_ref, gf_ref, qb_ref, kb_ref, vb_ref, gb_ref, hf_ref, hb_ref, cn_scr, m_scr):
    @pl.when(pl.program_id(1) == 0)
    def _():
        cn_scr[...] = jnp.zeros_like(cn_scr)
        m_scr[...] = jnp.zeros_like(m_scr)

    dirs = (0, 1)
    q_refs, k_refs, v_refs = (qf_ref, qb_ref), (kf_ref, kb_ref), (vf_ref, vb_ref)
    g_refs, h_refs = (gf_ref, gb_ref), (hf_ref, hb_ref)
    L = ML_CHUNK
    n_sub = qf_ref.shape[0] // L
    nh = MLSTM_HEADS
    i_off = (0, 2 * nh)
    last = (L - 1, 0)
    r_i = lax.broadcasted_iota(jnp.int32, (L, L), 0)
    c_i = lax.broadcasted_iota(jnp.int32, (L, L), 1)
    tri = (c_i <= r_i, c_i >= r_i)
    ones = jnp.ones((L, LANES), BF16)
    row = lax.broadcasted_iota(jnp.int32, (LANES, L), 0)
    hc = lambda hd: slice(hd * LANES, (hd + 1) * LANES)
    rows_of = lambda sub: [slice(sub * L, (sub + 1) * L), slice((n_sub - 1 - sub) * L, (n_sub - sub) * L)]

    m_prev = [m_scr[d] for d in dirs]
    gates = []
    for sub in range(n_sub):
        rows = rows_of(sub)
        g = [g_refs[d][rows[d], :] for d in dirs]
        b_f = [jnp.dot(tri[d].astype(F32), _log_sigmoid(g[d]), precision=lax.Precision.HIGHEST,
                       preferred_element_type=F32) for d in dirs]
        b = [pltpu.roll(b_f[d], LANES - nh, 1) for d in dirs]
        u = [g[d] - b[d] for d in dirs]
        cmax = [_running_max(u[d], d == 1) for d in dirs]
        mm = [jnp.maximum(m_prev[d], cmax[d]) for d in dirs]
        em = [jnp.exp(-(b[d] + mm[d])) for d in dirs]
        b_last = [b[d][last[d]:last[d] + 1, :] for d in dirs]
        a = [b_last[d] - b[d] + g[d] for d in dirs]
        m_new = [jnp.maximum(b_last[d] + m_prev[d], jnp.max(a[d], axis=0, keepdims=True)) for d in dirs]
        decay = [jnp.exp(b_last[d] + m_prev[d] - m_new[d]) for d in dirs]
        w = [jnp.exp(a[d] - m_new[d]) for d in dirs]
        u_t = [u[d].T for d in dirs]
        w_t = [w[d].T for d in dirs]
        gates.append((rows, m_prev, mm, em, decay, u_t, w_t))
        m_prev = m_new
    for d in dirs:
        m_scr[d] = m_prev[d]

    for sub in range(n_sub):
        rows, m_prev, mm, em, decay, u_t, w_t = gates[sub]
        for pair in range(nh // 2):
            pc = slice(pair * LANES, (pair + 1) * LANES)
            qp = [q_refs[d][rows[d], pc] for d in dirs]
            kt = [k_refs[d][rows[d], pc].astype(F32).T for d in dirs]
            ch = [(d, 2 * pair + half, i_off[d] + 2 * pair + half, half) for half in range(2) for d in dirs]
            kt_h = [jnp.where((row < MLSTM_QK) if half == 0 else (row >= MLSTM_QK), kt[d], 0.0)
                    for d, hd, col, half in ch]
            v1 = [jnp.concatenate([v_refs[d][rows[d], hc(hd)], ones], axis=1) for d, hd, col, half in ch]
            mm_b = [jnp.broadcast_to(mm[d][:, col:col + 1], (L, LANES)) for d, hd, col, half in ch]
            em_b = [jnp.broadcast_to(em[d][:, col:col + 1], (L, LANES)) for d, hd, col, half in ch]
            dw = [jnp.where(tri[d], jnp.exp(u_t[d][col:col + 1, :] - mm_b[i]), 0.0) for i, (d, hd, col, half) in enumerate(ch)]
            s = [_dot(qp[d], kt_h[i].astype(BF16)) * dw[i] for i, (d, hd, col, half) in enumerate(ch)]
            cn = [cn_scr[d, hd] for d, hd, col, half in ch]
            qc = [_dot(qp[d], cn[i].astype(BF16)) for i, (d, hd, col, half) in enumerate(ch)]
            sv = [_dot(s[i].astype(BF16), v1[i]) for i in range(len(ch))]
            w_inter = [jnp.exp(m_prev[d][:, col:col + 1] - mm_b[i]) for i, (d, hd, col, half) in enumerate(ch)]
            num = [w_inter[i] * qc[i][:, :LANES] + sv[i][:, :LANES] for i in range(len(ch))]
            den = [w_inter[i] * qc[i][:, LANES:] + sv[i][:, LANES:] for i in range(len(ch))]
            for i, (d, hd, col, half) in enumerate(ch):
                h_refs[d][rows[d], hc(hd)] = num[i] * (1.0 / jnp.maximum(jnp.abs(den[i]), em_b[i]))
            ktw = [(kt_h[i] * w_t[d][col:col + 1, :]).astype(BF16) for i, (d, hd, col, half) in enumerate(ch)]
            upd = [_dot(ktw[i], v1[i]) for i in range(len(ch))]
            for i, (d, hd, col, half) in enumerate(ch):
                cn_scr[d, hd] = decay[d][:, col:col + 1] * cn[i] + upd[i]


def _mlstm(QK, V, G, *, ctx_len):
    nb, t, nqk = QK.shape
    nbl = nb - 1
    L = ML_CHUNK * ML_CHUNKS_PER_STEP
    assert ctx_len % L == 0
    ncc, nlc = ctx_len // L, t // L
    nq = nqk // 2
    nv = V.shape[-1]

    def spec(width, col, rev):
        def index(b, c):
            cc, lc = (ncc - 1 - c, nlc - 1 - (c - ncc)) if rev else (c, c - ncc)
            is_ctx = c < ncc
            return jnp.where(is_ctx, nbl, b), jnp.where(is_ctx, b * ncc + cc, lc), col
        return pl.BlockSpec((None, L, width), index)

    ins = lambda rev: [spec(nq, 0, rev), spec(nq, 1, rev), spec(nv, 0, rev), spec(LANES, 0, rev)]
    out = jax.ShapeDtypeStruct((nb, t, nv), F32)
    return pl.pallas_call(
        _mlstm_kernel,
        out_shape=[out, out],
        grid=(nbl, ncc + nlc),
        in_specs=ins(False) + ins(True),
        out_specs=[spec(nv, 0, False), spec(nv, 0, True)],
        scratch_shapes=[pltpu.VMEM((2, MLSTM_HEADS, LANES, MLSTM_V + LANES), F32),
                        pltpu.VMEM((2, 1, LANES), F32)],
        compiler_params=_cparams(("parallel", "arbitrary")),
        name="mlstm",
    )(QK, QK, V, G, QK, QK, V, G)


def kernel(x, c, ctx, c_ctx, ada_w, ada_b, norm1_g, norm2_g, ffn_w_up, ffn_conv_w, ffn_conv_b, ffn_w_down,
           att_w_in, mla_qa_g, mla_w_qb, mla_kva_g, mla_w_kvb, mla_q_g, mla_k_g, gqa_q_g, gqa_k_g, att_w_out,
           ml_w_in, ml_conv_w, ml_conv_b, ml_gate_b, ml_out_g, ml_w_out):
    B, T, D = x.shape
    ctx_len = ctx.shape[1]
    depth = ada_w.shape[0]
    assert B * ctx_len == T, "context sequences must tile one latent-length row of the token array"
    NB = B + 1

    tok = _Tokens(x, ctx.reshape(1, T, D))

    rows = -(-NB // SUBLANES) * SUBLANES
    cvec = jnp.zeros((rows, D), F32).at[:B].set(c).at[B].set(c_ctx)
    mod = _adaln(cvec, ada_w, ada_b)
    mod = mod.reshape(depth, rows, 6, 1, D).transpose(0, 2, 1, 3, 4)

    rope_m = _rope_tables(T, MLA_ROPE, MLA_NOPE)
    rope_g = _rope_tables(T, GQA_DIM, 0, GQA_DIM)

    w_up_all, w_down_all = ffn_w_up.astype(BF16), ffn_w_down.astype(BF16)
    for layer in range(depth):
        last = layer == depth - 1
        j = layer // 2
        sh1, sc1, gt1, sh2, sc2, gt2 = (mod[layer, i] for i in range(6))
        g1 = norm1_g[layer][None, :]
        g2 = norm2_g[layer][None, :]
        if layer % 2 == 0:
            ew = _even_weights(att_w_in[j], mla_w_qb[j], mla_w_kvb[j], mla_q_g[j], mla_k_g[j],
                               gqa_q_g[j], gqa_k_g[j])
            qm, km, vm, qg, kg, vg = _even_in(tok, sh1, sc1, g1, ew, mla_qa_g[j][None, :], mla_kva_g[j][None, :],
                                              rope_m, rope_g)
            oa = _attention(qm, km, vm, n_heads=MLA_HEADS, q_pack=1, k_share=1, k_variants=1, v_share=2, ctx_len=ctx_len)
            rep = GQA_HEADS // GQA_KV_HEADS
            ob = _attention(qg, kg, vg, n_heads=GQA_HEADS, q_pack=2, k_share=rep, k_variants=2, v_share=rep,
                            ctx_len=ctx_len)
            mixer, mix_in, out_g, w_out = "attn", (oa, ob), None, att_w_out[j]
        else:
            assert tok.ctx is None, "an odd layer reads the assembled token array"
            gate_b = _pad_cols(ml_gate_b[j][None, :], LANES)
            w_in = jnp.pad(ml_w_in[j].astype(BF16), ((0, 0), (0, LANES - 4 * MLSTM_HEADS)))
            qk, v, o, gates = _odd_in(tok.lat, sh1, sc1, g1, w_in, gate_b, ml_conv_w[j], ml_conv_b[j][None, :],
                                      ctx_len=ctx_len)
            hf, hb = _mlstm(qk, v, gates, ctx_len=ctx_len)
            mixer, mix_in, out_g, w_out = "mlstm", (hf, hb, o), ml_out_g[j].reshape(1, -1), ml_w_out[j]
        X = _mix_ffn(mixer, tok, mix_in, out_g, w_out.astype(BF16), gt1, sh2, sc2, gt2, g2,
                     w_up_all, ffn_conv_w, ffn_conv_b[:, None, :], w_down_all,
                     layer=layer, n_out_batches=B if last else NB, n_lat_batches=B, ctx_len=ctx_len)
        tok = _Tokens(X)
    return X[:B]
```
